```python
import jax
import jax.numpy as jnp
from jax import lax
import numpy as np

D_MODEL = 1024
BATCH = 2
SEQ = 8192
DEPTH = 2
DEC_BATCH = 128
DEC_SEQ = 4
PAST_LEN = 16384
PAGE_SIZE = 128

HEAD_DIM = 64
MOBA_HEADS = 6
MOBA_KV_HEADS = 2
MOBA_GROUP = MOBA_HEADS // MOBA_KV_HEADS
MOBA_BLOCK = 256
MOBA_TOPK = 3
SB_HEADS = 4
MLA_HEADS = 6
MLA_Q_LORA = 256
MLA_KV_LORA = 256
MLA_NOPE = 64
MLA_ROPE = 32
MLA_V = 64
ROPE_THETA = 10000.0
MIX_A = MOBA_HEADS * HEAD_DIM
MIX_B = SB_HEADS * HEAD_DIM
MIX_C = MLA_HEADS * MLA_V
MIX_WIDTH = MIX_A + MIX_B + MIX_C
D_FF = 2816
CONV_W = 3
Q_BLOCK = 128
RMS_EPS = 1e-6
IN_SIZES = (MIX_A, MOBA_KV_HEADS * HEAD_DIM, MOBA_KV_HEADS * HEAD_DIM, MIX_B, HEAD_DIM, HEAD_DIM, MLA_Q_LORA, MLA_KV_LORA, MLA_ROPE)
IN_WIDTH = sum(IN_SIZES)
IN_SPLITS = tuple(np.cumsum(IN_SIZES)[:-1].tolist())

kernel_name = 'hybrid_moba_stickbreak_mla_convffn_step'


def _rms(x, g=None):
    xf = x.astype(jnp.float32)
    y = xf * lax.rsqrt(jnp.mean(xf * xf, axis=-1, keepdims=True) + RMS_EPS)
    if g is not None:
        y = y * g.astype(jnp.float32)
    return y.astype(x.dtype)


def _rope(x, pos):
    half = MLA_ROPE // 2
    inv = ROPE_THETA ** (-jnp.arange(half, dtype=jnp.float32) / half)
    ang = pos.astype(jnp.float32)[:, None] * inv[None, :]
    ang = ang.reshape((ang.shape[0],) + (1,) * (x.ndim - 3) + (half,))
    cos, sin = jnp.cos(ang), jnp.sin(ang)
    xf = x.astype(jnp.float32)
    x1, x2 = xf[..., :half], xf[..., half:]
    return jnp.concatenate([x1 * cos - x2 * sin, x1 * sin + x2 * cos], axis=-1).astype(x.dtype)


def _alibi_slopes():
    i = jnp.arange(1, MOBA_HEADS + 1, dtype=jnp.float32)
    return jnp.exp2(-8.0 * i / MOBA_HEADS).reshape(MOBA_KV_HEADS, MOBA_GROUP)


def _project(h, pos, l, p):
    n, s, _ = h.shape
    proj = h @ p['w_in'][l]
    qa, ka, va, qb, kb, vb, cq, ckv, kpe = jnp.split(proj, IN_SPLITS, axis=-1)
    qa = qa.reshape(n, s, MOBA_KV_HEADS, MOBA_GROUP, HEAD_DIM).transpose(0, 2, 3, 1, 4)
    ka = ka.reshape(n, s, MOBA_KV_HEADS, HEAD_DIM)
    va = va.reshape(n, s, MOBA_KV_HEADS, HEAD_DIM)
    qb = qb.reshape(n, s, SB_HEADS, HEAD_DIM).transpose(0, 2, 1, 3)
    q = (_rms(cq, p['mla_q_norm_g'][l]) @ p['mla_w_uq'][l]).reshape(n, s, MLA_HEADS, MLA_NOPE + MLA_ROPE)
    q_lat = jnp.einsum('nshd,rhd->nhsr', q[..., :MLA_NOPE], p['mla_w_uk'][l])
    q_pe = _rope(q[..., MLA_NOPE:], pos).transpose(0, 2, 1, 3)
    ckv = _rms(ckv, p['mla_kv_norm_g'][l])
    kpe = _rope(kpe, pos)
    return qa, ka, va, qb, kb, vb, q_lat, q_pe, ckv, kpe


def _moba_select(q, means, n_past_blocks):
    gate = jnp.einsum('nhgqd,njhd->nhgqj', q, means).astype(jnp.float32)
    nb = means.shape[1]
    allowed = jnp.arange(nb)[None, :] < n_past_blocks[:, None]
    gate = jnp.where(allowed, gate, -jnp.inf)
    k_sel = min(MOBA_TOPK, nb)
    _, idx = lax.top_k(gate, k_sel)
    valid = jnp.arange(k_sel)[None, :] < n_past_blocks[:, None]
    return idx, valid


def _moba_core(q, qpos, own_k, own_v, own_pos, sel):
    scale = HEAD_DIM ** -0.5
    slopes = _alibi_slopes()[:, :, None, None]
    d_own = (qpos[:, None] - own_pos[None, :]).astype(jnp.float32)
    s_own = jnp.einsum('nhgqd,nkhd->nhgqk', q, own_k).astype(jnp.float32) * scale - slopes * d_own
    s_own = jnp.where(d_own >= 0, s_own, -jnp.inf)
    if sel is None:
        pr = jax.nn.softmax(s_own, axis=-1)
        return jnp.einsum('nhgqk,nkhd->nhgqd', pr, own_v).astype(q.dtype)
    k_sel, v_sel, idx, valid = sel
    sel_pos = idx[..., None] * MOBA_BLOCK + jnp.arange(MOBA_BLOCK)
    d_sel = (qpos[:, None, None] - sel_pos).astype(jnp.float32)
    s_sel = jnp.einsum('nhgqd,nhgqjkd->nhgqjk', q, k_sel).astype(jnp.float32) * scale - slopes[..., None] * d_sel
    s_sel = jnp.where(valid[:, :, None], s_sel, -jnp.inf)
    n, hk, g, nq, j, blk = s_sel.shape
    pr = jax.nn.softmax(jnp.concatenate([s_sel.reshape(n, hk, g, nq, j * blk), s_own], axis=-1), axis=-1)
    p_sel = pr[..., :j * blk].reshape(n, hk, g, nq, j, blk)
    p_own = pr[..., j * blk:]
    out = jnp.einsum('nhgqjk,nhgqjkd->nhgqd', p_sel, v_sel) + jnp.einsum('nhgqk,nkhd->nhgqd', p_own, own_v)
    return out.astype(q.dtype)


def _sb_core(q, qpos, k, v, kpos):
    z = jnp.einsum('nhqd,nkd->nhqk', q, k).astype(jnp.float32) * (HEAD_DIM ** -0.5)
    mask = kpos[None, :] < qpos[:, None]
    log_1m = jnp.where(mask, jax.nn.log_sigmoid(-z), 0.0)
    tail = lax.cumsum(log_1m, axis=z.ndim - 1, reverse=True) - log_1m
    a = jnp.where(mask, jnp.exp(jax.nn.log_sigmoid(z) + tail), 0.0)
    return jnp.einsum('nhqk,nkd->nhqd', a, v).astype(q.dtype)


def _mla_core(q_lat, q_pe, qpos, ckv, kpe, kpos, w_uv):
    scale = (MLA_NOPE + MLA_ROPE) ** -0.5
    s = (jnp.einsum('nhqr,nkr->nhqk', q_lat, ckv) + jnp.einsum('nhqe,nke->nhqk', q_pe, kpe)).astype(jnp.float32) * scale
    s = jnp.where(kpos[None, :] <= qpos[:, None], s, -jnp.inf)
    pr = jax.nn.softmax(s, axis=-1)
    o_lat = jnp.einsum('nhqk,nkr->nhqr', pr, ckv)
    out = jnp.einsum('nhqr,rhd->nqhd', o_lat, w_uv)
    n, nq = out.shape[:2]
    return out.reshape(n, nq, MIX_C).astype(q_lat.dtype)


def _moba_prompt(q, k, v):
    b, s = k.shape[:2]
    nbf = s // MOBA_BLOCK
    pad = (-s) % MOBA_BLOCK
    kp = jnp.pad(k, ((0, 0), (0, pad), (0, 0), (0, 0)))
    vp = jnp.pad(v, ((0, 0), (0, pad), (0, 0), (0, 0)))
    if nbf > 0:
        kfull = k[:, :nbf * MOBA_BLOCK].reshape(b, nbf, MOBA_BLOCK, MOBA_KV_HEADS, HEAD_DIM)
        vfull = v[:, :nbf * MOBA_BLOCK].reshape(b, nbf, MOBA_BLOCK, MOBA_KV_HEADS, HEAD_DIM)
        means = jnp.mean(kfull, axis=2, dtype=jnp.float32).astype(k.dtype)
        kblk = kfull.transpose(0, 3, 1, 2, 4)
        vblk = vfull.transpose(0, 3, 1, 2, 4)
        bi = jnp.arange(b)[:, None, None, None, None]
        hi = jnp.arange(MOBA_KV_HEADS)[None, :, None, None, None]

    def block(c):
        q0 = c * Q_BLOCK
        qc = lax.dynamic_slice_in_dim(q, q0, Q_BLOCK, axis=3)
        qpos = q0 + jnp.arange(Q_BLOCK)
        own0 = (q0 // MOBA_BLOCK) * MOBA_BLOCK
        own_k = lax.dynamic_slice_in_dim(kp, own0, MOBA_BLOCK, axis=1)
        own_v = lax.dynamic_slice_in_dim(vp, own0, MOBA_BLOCK, axis=1)
        own_pos = own0 + jnp.arange(MOBA_BLOCK)
        sel = None
        if nbf > 0:
            idx, valid = _moba_select(qc, means, qpos // MOBA_BLOCK)
            sel = (kblk[bi, hi, idx], vblk[bi, hi, idx], idx, valid)
        return _moba_core(qc, qpos, own_k, own_v, own_pos, sel)

    out = lax.map(block, jnp.arange(s // Q_BLOCK))
    return out.transpose(1, 0, 4, 2, 3, 5).reshape(b, s, MIX_A)


def _sb_prompt(q, k, v):
    b, s = k.shape[:2]
    kpos = jnp.arange(s)

    def block(c):
        q0 = c * Q_BLOCK
        qc = lax.dynamic_slice_in_dim(q, q0, Q_BLOCK, axis=2)
        return _sb_core(qc, q0 + jnp.arange(Q_BLOCK), k, v, kpos)

    out = lax.map(block, jnp.arange(s // Q_BLOCK))
    return out.transpose(1, 0, 3, 2, 4).reshape(b, s, MIX_B)


def _mla_prompt(q_lat, q_pe, ckv, kpe, w_uv):
    b, s = ckv.shape[:2]
    kpos = jnp.arange(s)

    def block(c):
        q0 = c * Q_BLOCK
        ql = lax.dynamic_slice_in_dim(q_lat, q0, Q_BLOCK, axis=2)
        qr = lax.dynamic_slice_in_dim(q_pe, q0, Q_BLOCK, axis=2)
        return _mla_core(ql, qr, q0 + jnp.arange(Q_BLOCK), ckv, kpe, kpos, w_uv)

    out = lax.map(block, jnp.arange(s // Q_BLOCK))
    return out.transpose(1, 0, 2, 3).reshape(b, s, MIX_C)


def _moba_sample(q, k_new, v_new, pool_k, pool_v, page_table, l):
    n, nq = k_new.shape[:2]
    past = page_table.shape[1] * PAGE_SIZE
    ppb = MOBA_BLOCK // PAGE_SIZE
    nbp = past // MOBA_BLOCK
    own0 = nbp * MOBA_BLOCK
    qpos = past + jnp.arange(nq)
    own_pages = page_table[:, own0 // PAGE_SIZE: past // PAGE_SIZE]
    n_own_past = past - own0
    own_k = jnp.concatenate([pool_k[l, own_pages].reshape(n, n_own_past, MOBA_KV_HEADS, HEAD_DIM), k_new], axis=1)
    own_v = jnp.concatenate([pool_v[l, own_pages].reshape(n, n_own_past, MOBA_KV_HEADS, HEAD_DIM), v_new], axis=1)
    own_pos = own0 + jnp.arange(n_own_past + nq)
    sel = None
    if nbp > 0:
        k_past = pool_k[l, page_table[:, :nbp * ppb]].reshape(n, nbp, MOBA_BLOCK, MOBA_KV_HEADS, HEAD_DIM)
        means = jnp.mean(k_past, axis=2, dtype=jnp.float32).astype(k_new.dtype)
        idx, valid = _moba_select(q, means, qpos // MOBA_BLOCK)
        logical = idx[..., None] * ppb + jnp.arange(ppb)
        phys = page_table[jnp.arange(n)[:, None, None, None, None, None], logical]
        hi = jnp.arange(MOBA_KV_HEADS)[None, :, None, None, None, None]
        shp = idx.shape + (MOBA_BLOCK, HEAD_DIM)
        sel = (pool_k[l, phys, :, hi].reshape(shp), pool_v[l, phys, :, hi].reshape(shp), idx, valid)
    out = _moba_core(q, qpos, own_k, own_v, own_pos, sel)
    return out.transpose(0, 3, 1, 2, 4).reshape(n, nq, MIX_A)


def _sb_sample(q, k_new, v_new, pool_k, pool_v, page_table, l):
    n, nq = k_new.shape[:2]
    past = page_table.shape[1] * PAGE_SIZE
    k = jnp.concatenate([pool_k[l, page_table].reshape(n, past, HEAD_DIM), k_new], axis=1)
    v = jnp.concatenate([pool_v[l, page_table].reshape(n, past, HEAD_DIM), v_new], axis=1)
    out = _sb_core(q, past + jnp.arange(nq), k, v, jnp.arange(past + nq))
    return out.transpose(0, 2, 1, 3).reshape(n, nq, MIX_B)


def _mla_sample(q_lat, q_pe, ckv_new, kpe_new, pool_c, pool_r, page_table, l, w_uv):
    n, nq = ckv_new.shape[:2]
    past = page_table.shape[1] * PAGE_SIZE
    ckv = jnp.concatenate([pool_c[l, page_table].reshape(n, past, MLA_KV_LORA), ckv_new], axis=1)
    kpe = jnp.concatenate([pool_r[l, page_table].reshape(n, past, MLA_ROPE), kpe_new], axis=1)
    return _mla_core(q_lat, q_pe, past + jnp.arange(nq), ckv, kpe, jnp.arange(past + nq), w_uv)


def _merge(oa, ob, oc, l, p):
    o = jnp.concatenate([_rms(oa), _rms(ob), _rms(oc)], axis=-1) * p['out_norm_g'][l]
    return o @ p['w_out'][l]


def _conv_ffn(h, buf, l, p):
    g = h @ p['ffn_w_gate'][l]
    u = h @ p['ffn_w_up'][l]
    ext = jnp.concatenate([buf.astype(g.dtype), g], axis=1)
    s = g.shape[1]
    w = p['ffn_conv_w'][l]
    conv = p['ffn_conv_b'][l] + ext[:, 0:s] * w[0]
    for i in range(1, CONV_W):
        conv = conv + ext[:, i:i + s] * w[i]
    y = (jax.nn.silu(conv) * u) @ p['ffn_w_down'][l]
    return y, ext[:, s:]


def _prompt_forward(x, p):
    b, s, _ = x.shape
    pos = jnp.arange(s)
    buf0 = jnp.zeros((b, CONV_W - 1, D_FF), x.dtype)
    mk, mv, sk, sv, mc, mr, fc = [], [], [], [], [], [], []
    for l in range(DEPTH):
        h = _rms(x, p['norm1_g'][l])
        qa, ka, va, qb, kb, vb, q_lat, q_pe, ckv, kpe = _project(h, pos, l, p)
        oa = _moba_prompt(qa, ka, va)
        ob = _sb_prompt(qb, kb, vb)
        oc = _mla_prompt(q_lat, q_pe, ckv, kpe, p['mla_w_uv'][l])
        x = x + _merge(oa, ob, oc, l, p)
        y, buf = _conv_ffn(_rms(x, p['norm2_g'][l]), buf0, l, p)
        x = x + y
        mk.append(ka); mv.append(va); sk.append(kb); sv.append(vb)
        mc.append(ckv); mr.append(kpe); fc.append(buf)
    st = [jnp.stack(a, axis=0) for a in (mk, mv, sk, sv, mc, mr, fc)]
    return _rms(x, p['final_norm_g']), st


def _sample_forward(x, c_mk, c_mv, c_sk, c_sv, c_mc, c_mr, s_conv, page_table, p):
    n, nq, _ = x.shape
    past = page_table.shape[1] * PAGE_SIZE
    pos = past + jnp.arange(nq)
    mk, mv, sk, sv, mc, mr, fc = [], [], [], [], [], [], []
    for l in range(DEPTH):
        h = _rms(x, p['norm1_g'][l])
        qa, ka, va, qb, kb, vb, q_lat, q_pe, ckv, kpe = _project(h, pos, l, p)
        oa = _moba_sample(qa, ka, va, c_mk, c_mv, page_table, l)
        ob = _sb_sample(qb, kb, vb, c_sk, c_sv, page_table, l)
        oc = _mla_sample(q_lat, q_pe, ckv, kpe, c_mc, c_mr, page_table, l, p['mla_w_uv'][l])
        x = x + _merge(oa, ob, oc, l, p)
        y, buf = _conv_ffn(_rms(x, p['norm2_g'][l]), s_conv[l], l, p)
        x = x + y
        mk.append(ka); mv.append(va); sk.append(kb); sv.append(vb)
        mc.append(ckv); mr.append(kpe); fc.append(buf)
    st = [jnp.stack(a, axis=0) for a in (mk, mv, sk, sv, mc, mr, fc)]
    return _rms(x, p['final_norm_g']), st


def setup_inputs(seed: int = 0) -> dict:
    key = jax.random.key(seed)
    ks = jax.random.split(key, 32)
    n_pages = PAST_LEN // PAGE_SIZE
    n_used = DEC_BATCH * n_pages
    n_pool = n_used + max(1, n_used // 4)
    f32 = jnp.float32

    def nrm(k, shape, scale=1.0):
        return jax.random.normal(k, shape, f32) * scale

    def gain(k, shape):
        return 1.0 + 0.02 * jax.random.normal(k, shape, f32)

    page_table = jax.random.permutation(ks[0], n_pool)[:n_used].reshape(DEC_BATCH, n_pages).astype(jnp.int32)
    return {
        'x_prompt': nrm(ks[1], (BATCH, SEQ, D_MODEL)),
        'x_sample': nrm(ks[2], (DEC_BATCH, DEC_SEQ, D_MODEL)),
        'cache_moba_k': nrm(ks[3], (DEPTH, n_pool, PAGE_SIZE, MOBA_KV_HEADS, HEAD_DIM)),
        'cache_moba_v': nrm(ks[4], (DEPTH, n_pool, PAGE_SIZE, MOBA_KV_HEADS, HEAD_DIM)),
        'cache_sb_k': nrm(ks[5], (DEPTH, n_pool, PAGE_SIZE, HEAD_DIM)),
        'cache_sb_v': nrm(ks[6], (DEPTH, n_pool, PAGE_SIZE, HEAD_DIM)),
        'cache_mla_latent': nrm(ks[7], (DEPTH, n_pool, PAGE_SIZE, MLA_KV_LORA)),
        'cache_mla_krope': nrm(ks[8], (DEPTH, n_pool, PAGE_SIZE, MLA_ROPE)),
        'state_ffn_conv': nrm(ks[9], (DEPTH, DEC_BATCH, CONV_W - 1, D_FF)),
        'page_table': page_table,
        'norm1_g': gain(ks[10], (DEPTH, D_MODEL)),
        'w_in': nrm(ks[11], (DEPTH, D_MODEL, IN_WIDTH), D_MODEL ** -0.5),
        'mla_q_norm_g': gain(ks[12], (DEPTH, MLA_Q_LORA)),
        'mla_w_uq': nrm(ks[13], (DEPTH, MLA_Q_LORA, MLA_HEADS * (MLA_NOPE + MLA_ROPE)), MLA_Q_LORA ** -0.5),
        'mla_kv_norm_g': gain(ks[14], (DEPTH, MLA_KV_LORA)),
        'mla_w_uk': nrm(ks[15], (DEPTH, MLA_KV_LORA, MLA_HEADS, MLA_NOPE), MLA_KV_LORA ** -0.5),
        'mla_w_uv': nrm(ks[16], (DEPTH, MLA_KV_LORA, MLA_HEADS, MLA_V), MLA_KV_LORA ** -0.5),
        'out_norm_g': gain(ks[17], (DEPTH, MIX_WIDTH)),
        'w_out': nrm(ks[18], (DEPTH, MIX_WIDTH, D_MODEL), MIX_WIDTH ** -0.5),
        'norm2_g': gain(ks[19], (DEPTH, D_MODEL)),
        'ffn_w_gate': nrm(ks[20], (DEPTH, D_MODEL, D_FF), D_MODEL ** -0.5),
        'ffn_w_up': nrm(ks[21], (DEPTH, D_MODEL, D_FF), D_MODEL ** -0.5),
        'ffn_conv_w': nrm(ks[22], (DEPTH, CONV_W, D_FF), CONV_W ** -0.5),
        'ffn_conv_b': nrm(ks[23], (DEPTH, D_FF), 0.01),
        'ffn_w_down': nrm(ks[24], (DEPTH, D_FF, D_MODEL), D_FF ** -0.5),
        'final_norm_g': gain(ks[25], (D_MODEL,)),
    }


def reference(x_prompt, x_sample, cache_moba_k, cache_moba_v, cache_sb_k, cache_sb_v, cache_mla_latent, cache_mla_krope, state_ffn_conv, page_table, norm1_g, w_in, mla_q_norm_g, mla_w_uq, mla_kv_norm_g, mla_w_uk, mla_w_uv, out_norm_g, w_out, norm2_g, ffn_w_gate, ffn_w_up, ffn_conv_w, ffn_conv_b, ffn_w_down, final_norm_g):
    p = dict(norm1_g=norm1_g, w_in=w_in, mla_q_norm_g=mla_q_norm_g, mla_w_uq=mla_w_uq,
             mla_kv_norm_g=mla_kv_norm_g, mla_w_uk=mla_w_uk, mla_w_uv=mla_w_uv, out_norm_g=out_norm_g,
             w_out=w_out, norm2_g=norm2_g, ffn_w_gate=ffn_w_gate, ffn_w_up=ffn_w_up,
             ffn_conv_w=ffn_conv_w, ffn_conv_b=ffn_conv_b, ffn_w_down=ffn_w_down, final_norm_g=final_norm_g)
    y_prompt, pst = _prompt_forward(x_prompt, p)
    p_moba_k, p_moba_v, p_sb_k, p_sb_v, p_mla_latent, p_mla_krope, p_ffn_conv = pst
    y_sample, sst = _sample_forward(x_sample, cache_moba_k, cache_moba_v, cache_sb_k, cache_sb_v,
                                    cache_mla_latent, cache_mla_krope, state_ffn_conv, page_table, p)
    s_moba_k, s_moba_v, s_sb_k, s_sb_v, s_mla_latent, s_mla_krope, s_ffn_conv = sst
    return (y_prompt, y_sample, p_moba_k, p_moba_v, p_sb_k, p_sb_v, p_mla_latent, p_mla_krope, p_ffn_conv, s_moba_k, s_moba_v, s_sb_k, s_sb_v, s_mla_latent, s_mla_krope, s_ffn_conv)
```

```python
import functools

import jax
import jax.numpy as jnp
from jax import lax
from jax.experimental import pallas as pl
from jax.experimental.pallas import tpu as pltpu

F32 = jnp.float32
BF16 = jnp.bfloat16
HIGHEST = lax.Precision.HIGHEST

HEAD_DIM = 64
MOBA_HEADS = 6
MOBA_KV_HEADS = 2
MOBA_GROUP = MOBA_HEADS // MOBA_KV_HEADS
MOBA_BLOCK = 256
MOBA_TOPK = 3
SB_HEADS = 4
MLA_HEADS = 6
MLA_Q_LORA = 256
MLA_KV_LORA = 256
MLA_NOPE = 64
MLA_ROPE = 32
MLA_QK = MLA_NOPE + MLA_ROPE
MLA_V = 64
ROPE_THETA = 10000.0
MIX_A = MOBA_HEADS * HEAD_DIM
MIX_B = SB_HEADS * HEAD_DIM
MIX_C = MLA_HEADS * MLA_V
CONV_W = 3
RMS_EPS = 1e-6
PAGE_SIZE = 128

V7X_VMEM_BYTES = 64 * 1024 * 1024
VMEM_LIMIT = (V7X_VMEM_BYTES * 7) // 8
LANES = 128

KEY_TILE = 256
MASK_NEG = -1e30
PAGES_PER_STEP = 8
HI_COLS = MIX_A + MOBA_KV_HEADS * HEAD_DIM


def _dot(a, b, precision=None):
    return jnp.dot(a, b, preferred_element_type=F32, precision=precision)


def _dot_t(a, b, precision=None):
    return lax.dot_general(a, b, (((1,), (1,)), ((), ())), preferred_element_type=F32, precision=precision)


def _rms(x, g):
    return x * lax.rsqrt(jnp.mean(x * x, axis=-1, keepdims=True) + RMS_EPS) * g


def _params(*sem):
    return pltpu.CompilerParams(dimension_semantics=sem, vmem_limit_bytes=VMEM_LIMIT)


def _const_spec(shape):
    nd = len(shape)
    return pl.BlockSpec(shape, lambda *_: (0,) * nd)


def _proj_kernel(*refs, absorbed):
    (x_ref, g1_ref, wah_ref, wal_ref, wb_ref, qng_ref, wq_ref, wqs_ref, kvg_ref, rc_ref, rs_ref) = refs[:11]
    if absorbed:
        (wukt_ref, qa_ref, ka_ref, va_ref, qb_ref, kb_ref, vb_ref, ckv_ref, kpe_ref, qlat_ref, qpe_ref) = refs[11:]
    else:
        (wk_ref, ekr_ref, wv_ref, qa_ref, ka_ref, ka2_ref, va_ref, va2_ref, qb_ref, kb_ref, vb_ref, ckv_ref, kpe_ref,
         mq_ref, mk_ref, mv_ref) = refs[11:]
    h = _rms(x_ref[...], g1_ref[...])
    h_hi = h.astype(BF16)
    h_lo = (h - h_hi.astype(F32)).astype(BF16)
    pa = _dot(h_hi, wah_ref[...]) + (_dot(h_lo, wah_ref[...]) + _dot(h_hi, wal_ref[...]))
    pb = _dot(h_hi, wb_ref[...])
    for k in range(MOBA_HEADS):
        qa_ref[k] = pa[:, k * HEAD_DIM:(k + 1) * HEAD_DIM] * (HEAD_DIM ** -0.5)
    ka = pa[:, MIX_A:MIX_A + 2 * HEAD_DIM]
    ka_ref[...] = ka
    va = pb[:, 0:128]
    va_ref[...] = va
    if not absorbed:
        for k in range(MOBA_KV_HEADS):
            ka2_ref[k] = ka[:, k * HEAD_DIM:(k + 1) * HEAD_DIM]
            va2_ref[k] = va[:, k * HEAD_DIM:(k + 1) * HEAD_DIM].astype(BF16)
    for k in range(SB_HEADS):
        qb_ref[k] = (pb[:, 128 + k * HEAD_DIM:128 + (k + 1) * HEAD_DIM] * (HEAD_DIM ** -0.5)).astype(BF16)
    kb_ref[...] = pb[:, 384:448]
    vb_ref[...] = pb[:, 448:512]
    cqn = _rms(pb[:, 512:768], qng_ref[...]).astype(BF16)
    ckvn = _rms(pb[:, 768:1024], kvg_ref[...])
    ckv_ref[...] = ckvn
    rc = rc_ref[...]
    rs = rs_ref[...]
    kr = pb[:, 1024:1056] * rc[:, MLA_NOPE:] + pb[:, 1056:1088] * rs[:, MLA_NOPE:]
    kpe_ref[...] = kr
    scale = MLA_QK ** -0.5
    if not absorbed:
        ckvn_b = ckvn.astype(BF16)
        kr_b = kr.astype(BF16)
    for hh in range(MLA_HEADS):
        qh = (_dot(cqn, wq_ref[hh]) * rc + _dot(cqn, wqs_ref[hh]) * rs) * scale
        if absorbed:
            qlat_ref[hh] = _dot(qh[:, :MLA_NOPE].astype(BF16), wukt_ref[hh])
            qpe_ref[hh] = qh[:, MLA_NOPE:]
        else:
            mq_ref[hh] = qh.astype(BF16)
            mk_ref[hh] = (_dot(ckvn_b, wk_ref[hh]) + _dot(kr_b, ekr_ref[...])).astype(BF16)
            mv_ref[hh] = _dot(ckvn_b, wv_ref[hh]).astype(BF16)


def _project(x2d, rc, rs, w, absorbed, tm):
    t, d = x2d.shape
    grid = (t // tm,)
    rows = lambda width: pl.BlockSpec((tm, width), lambda i: (i, 0))
    heads = lambda nh, width: pl.BlockSpec((nh, tm, width), lambda i: (0, i, 0))
    ins = [x2d, w['g1'], w['wa_hi'], w['wa_lo'], w['wb'], w['qn_g'], w['wq6'], w['wq6s'], w['kvn_g'], rc, rs]
    in_specs = [rows(d)] + [_const_spec(a.shape) for a in ins[1:9]] + [rows(MLA_QK), rows(MLA_QK)]
    if absorbed:
        extra = [w['wukt6']]
        outs = [((MOBA_HEADS, t, HEAD_DIM), F32, heads(MOBA_HEADS, HEAD_DIM)),
                ((t, 128), F32, rows(128)), ((t, 128), F32, rows(128)),
                ((SB_HEADS, t, HEAD_DIM), BF16, heads(SB_HEADS, HEAD_DIM)),
                ((t, HEAD_DIM), F32, rows(HEAD_DIM)), ((t, HEAD_DIM), F32, rows(HEAD_DIM)),
                ((t, MLA_KV_LORA), F32, rows(MLA_KV_LORA)), ((t, MLA_ROPE), F32, rows(MLA_ROPE)),
                ((MLA_HEADS, t, MLA_KV_LORA), F32, heads(MLA_HEADS, MLA_KV_LORA)),
                ((MLA_HEADS, t, MLA_ROPE), F32, heads(MLA_HEADS, MLA_ROPE))]
    else:
        extra = [w['wk6'], w['ekr'], w['wv6']]
        outs = [((MOBA_HEADS, t, HEAD_DIM), F32, heads(MOBA_HEADS, HEAD_DIM)),
                ((t, 128), F32, rows(128)), ((MOBA_KV_HEADS, t, HEAD_DIM), F32, heads(MOBA_KV_HEADS, HEAD_DIM)),
                ((t, 128), F32, rows(128)), ((MOBA_KV_HEADS, t, HEAD_DIM), BF16, heads(MOBA_KV_HEADS, HEAD_DIM)),
                ((SB_HEADS, t, HEAD_DIM), BF16, heads(SB_HEADS, HEAD_DIM)),
                ((t, HEAD_DIM), F32, rows(HEAD_DIM)), ((t, HEAD_DIM), F32, rows(HEAD_DIM)),
                ((t, MLA_KV_LORA), F32, rows(MLA_KV_LORA)), ((t, MLA_ROPE), F32, rows(MLA_ROPE)),
                ((MLA_HEADS, t, MLA_QK), BF16, heads(MLA_HEADS, MLA_QK)),
                ((MLA_HEADS, t, MLA_QK), BF16, heads(MLA_HEADS, MLA_QK)),
                ((MLA_HEADS, t, MLA_V), BF16, heads(MLA_HEADS, MLA_V))]
    ins += extra
    in_specs += [_const_spec(a.shape) for a in extra]
    return pl.pallas_call(
        functools.partial(_proj_kernel, absorbed=absorbed),
        grid=grid, in_specs=in_specs,
        out_specs=[o[2] for o in outs],
        out_shape=[jax.ShapeDtypeStruct(o[0], o[1]) for o in outs],
        compiler_params=_params("parallel"),
        name="proj_sample" if absorbed else "proj_prompt",
    )(*ins)


def _topk_mask(gate, allowed, col_f, k_sel):
    g = jnp.where(allowed, gate, -jnp.inf)
    sel = jnp.zeros(gate.shape, F32)
    for _ in range(k_sel):
        m = jnp.max(g, axis=1, keepdims=True)
        cand = jnp.where(g == m, col_f, 1e9)
        cand = jnp.where(m > -jnp.inf, cand, 1e9)
        first = jnp.min(cand, axis=1, keepdims=True)
        pick = col_f == first
        sel = jnp.where(pick, 1.0, sel)
        g = jnp.where(pick, -jnp.inf, g)
    return sel


def _moba_prompt_kernel(q_ref, k_ref, v_ref, slope_ref, d0_ref, o_ref, means_ref, kext_ref, *, tq, nb, ext):
    c = pl.program_id(2)
    rows = MOBA_GROUP * tq

    @pl.when(c == 0)
    def _build():
        means_ref[...] = jnp.zeros(means_ref.shape, F32)
        col = lax.broadcasted_iota(jnp.int32, (MOBA_BLOCK, ext), 1)

        def body(j, carry):
            r0 = pl.multiple_of(j * MOBA_BLOCK, MOBA_BLOCK)
            kb = k_ref[0, pl.ds(r0, MOBA_BLOCK), :]
            means_ref[pl.ds(j, 1), :] = jnp.sum(kb, axis=0, keepdims=True) * (1.0 / MOBA_BLOCK)
            onehot = jnp.where(col == j, 1.0, 0.0)
            kext_ref[pl.ds(r0, MOBA_BLOCK), :] = jnp.concatenate([kb, onehot], axis=1).astype(BF16)
            return carry

        lax.fori_loop(0, nb, body, 0)

    q = q_ref[...].reshape(rows, HEAD_DIM)
    n_past = (c * tq) // MOBA_BLOCK
    gate = _dot_t(q, means_ref[...], precision=HIGHEST)
    col_i = lax.broadcasted_iota(jnp.int32, (rows, ext), 1)
    sel = _topk_mask(gate, col_i < n_past, col_i.astype(F32), min(MOBA_TOPK, nb))
    keep = jnp.logical_or(sel > 0.5, col_i == n_past)
    q_ext = jnp.concatenate([q, jnp.where(keep, 0.0, MASK_NEG)], axis=1).astype(BF16)
    slope = slope_ref[0]
    d0 = d0_ref[...]

    def block(j, carry, causal):
        m, l, acc = carry
        r0 = pl.multiple_of(j * MOBA_BLOCK, MOBA_BLOCK)
        s = _dot_t(q_ext, kext_ref[pl.ds(r0, MOBA_BLOCK), :])
        d = d0 + (c * tq - j * MOBA_BLOCK).astype(F32)
        s = s - slope * d
        if causal:
            s = jnp.where(d >= 0, s, -jnp.inf)
        m_new = jnp.maximum(m, jnp.max(s, axis=1, keepdims=True))
        p = jnp.exp(s - m_new)
        alpha = jnp.exp(m - m_new)
        l = alpha * l + jnp.sum(p, axis=1, keepdims=True)
        acc = alpha * acc + _dot(p.astype(BF16), v_ref[0, pl.ds(r0, MOBA_BLOCK), :])
        return m_new, l, acc

    init = (jnp.full((rows, 1), -jnp.inf, F32), jnp.zeros((rows, 1), F32), jnp.zeros((rows, HEAD_DIM), F32))
    carry = lax.fori_loop(0, n_past, lambda j, cr: block(j, cr, False), init)
    _, l, acc = block(n_past, carry, True)
    o_ref[...] = (acc / l).reshape(MOBA_GROUP, tq, HEAD_DIM)


def _moba_prompt(qa6, ka2, va2, b, s, tq=128):
    nq = s // tq
    nb = s // MOBA_BLOCK
    ext = max(32, -(-nb // 32) * 32)
    rows = MOBA_GROUP * tq
    i = jnp.arange(1, MOBA_HEADS + 1, dtype=F32)
    slopes = jnp.exp2(-8.0 * i / MOBA_HEADS).reshape(MOBA_KV_HEADS, MOBA_GROUP)
    slope_full = jnp.broadcast_to(jnp.repeat(slopes, tq, axis=1)[:, :, None], (MOBA_KV_HEADS, rows, MOBA_BLOCK))
    qi = jnp.tile(jnp.arange(tq, dtype=F32), MOBA_GROUP)
    d0 = qi[:, None] - jnp.arange(MOBA_BLOCK, dtype=F32)[None, :]
    return pl.pallas_call(
        functools.partial(_moba_prompt_kernel, tq=tq, nb=nb, ext=ext),
        grid=(b, MOBA_KV_HEADS, nq),
        in_specs=[pl.BlockSpec((MOBA_GROUP, tq, HEAD_DIM), lambda bi, hi, ci: (hi, bi * nq + ci, 0)),
                  pl.BlockSpec((1, s, HEAD_DIM), lambda bi, hi, ci: (hi, bi, 0)),
                  pl.BlockSpec((1, s, HEAD_DIM), lambda bi, hi, ci: (hi, bi, 0)),
                  pl.BlockSpec((1, rows, MOBA_BLOCK), lambda bi, hi, ci: (hi, 0, 0)),
                  _const_spec((rows, MOBA_BLOCK))],
        out_specs=pl.BlockSpec((MOBA_GROUP, tq, HEAD_DIM), lambda bi, hi, ci: (hi, bi * nq + ci, 0)),
        out_shape=jax.ShapeDtypeStruct((MOBA_HEADS, b * s, HEAD_DIM), F32),
        scratch_shapes=[pltpu.VMEM((ext, HEAD_DIM), F32), pltpu.VMEM((s, HEAD_DIM + ext), BF16)],
        compiler_params=_params("parallel", "parallel", "arbitrary"),
        name="moba_prompt",
    )(qa6, ka2, va2, slope_full, d0)


def _sb_tile(z, mask, tri, r_run):
    sp = jnp.maximum(z, 0.0) + jnp.log1p(jnp.exp(-jnp.abs(z)))
    lm = -sp
    if mask is not None:
        lm = jnp.where(mask, lm, 0.0)
    tail = _dot(lm.astype(BF16), tri)
    a = jnp.exp((z - sp) + tail + r_run)
    if mask is not None:
        a = jnp.where(mask, a, 0.0)
    return a, r_run + jnp.sum(lm, axis=1, keepdims=True)


def _sb_prompt_kernel(q_ref, k_ref, v_ref, tri_ref, o_ref, kb_ref, vb_ref, *, tq):
    c = pl.program_id(1)
    rows = SB_HEADS * tq

    @pl.when(c == 0)
    def _cast():
        kb_ref[...] = k_ref[...].astype(BF16)
        vb_ref[...] = v_ref[...].astype(BF16)

    q = q_ref[...].reshape(rows, HEAD_DIM)
    tri = tri_ref[...]
    jd = (c * tq) // KEY_TILE

    def tile(j, carry, masked):
        r_run, acc = carry
        r0 = pl.multiple_of(j * KEY_TILE, KEY_TILE)
        z = _dot_t(q, kb_ref[pl.ds(r0, KEY_TILE), :])
        mask = None
        if masked:
            qpos = c * tq + (lax.broadcasted_iota(jnp.int32, (rows, KEY_TILE), 0) & (tq - 1))
            kpos = j * KEY_TILE + lax.broadcasted_iota(jnp.int32, (rows, KEY_TILE), 1)
            mask = kpos < qpos
        a, r_run = _sb_tile(z, mask, tri, r_run)
        acc = acc + _dot(a.astype(BF16), vb_ref[pl.ds(r0, KEY_TILE), :])
        return r_run, acc

    carry = tile(jd, (jnp.zeros((rows, 1), F32), jnp.zeros((rows, HEAD_DIM), F32)), True)
    _, acc = lax.fori_loop(0, jd, lambda i, cr: tile(jd - 1 - i, cr, False), carry)
    o_ref[...] = acc.reshape(SB_HEADS, tq, HEAD_DIM)


def _tri(n):
    r = jnp.arange(n)
    return (r[:, None] > r[None, :]).astype(BF16)


def _sb_prompt(qb4, kb, vb, b, s, tq=128):
    nq = s // tq
    return pl.pallas_call(
        functools.partial(_sb_prompt_kernel, tq=tq),
        grid=(b, nq),
        in_specs=[pl.BlockSpec((SB_HEADS, tq, HEAD_DIM), lambda bi, ci: (0, bi * nq + ci, 0)),
                  pl.BlockSpec((s, HEAD_DIM), lambda bi, ci: (bi, 0)),
                  pl.BlockSpec((s, HEAD_DIM), lambda bi, ci: (bi, 0)),
                  _const_spec((KEY_TILE, KEY_TILE))],
        out_specs=pl.BlockSpec((SB_HEADS, tq, HEAD_DIM), lambda bi, ci: (0, bi * nq + ci, 0)),
        out_shape=jax.ShapeDtypeStruct((SB_HEADS, b * s, HEAD_DIM), F32),
        scratch_shapes=[pltpu.VMEM((s, HEAD_DIM), BF16), pltpu.VMEM((s, HEAD_DIM), BF16)],
        compiler_params=_params("parallel", "arbitrary"),
        name="sb_prompt",
    )(qb4, kb, vb, _tri(KEY_TILE))


def _mla_prompt_kernel(q_ref, k_ref, v_ref, o_ref, *, tq):
    c = pl.program_id(2)
    q = q_ref[0]
    jd = (c * tq) // KEY_TILE

    def tile(j, carry, masked):
        m, l, acc = carry
        r0 = pl.multiple_of(j * KEY_TILE, KEY_TILE)
        s = _dot_t(q, k_ref[0, pl.ds(r0, KEY_TILE), :])
        if masked:
            qpos = c * tq + lax.broadcasted_iota(jnp.int32, (tq, KEY_TILE), 0)
            kpos = j * KEY_TILE + lax.broadcasted_iota(jnp.int32, (tq, KEY_TILE), 1)
            s = jnp.where(kpos <= qpos, s, -jnp.inf)
        m_new = jnp.maximum(m, jnp.max(s, axis=1, keepdims=True))
        p = jnp.exp(s - m_new)
        alpha = jnp.exp(m - m_new)
        l = alpha * l + jnp.sum(p, axis=1, keepdims=True)
        acc = alpha * acc + _dot(p.astype(BF16), v_ref[0, pl.ds(r0, KEY_TILE), :])
        return m_new, l, acc

    init = (jnp.full((tq, 1), -jnp.inf, F32), jnp.zeros((tq, 1), F32), jnp.zeros((tq, MLA_V), F32))
    carry = tile(jd, init, True)
    _, l, acc = lax.fori_loop(0, jd, lambda j, cr: tile(j, cr, False), carry)
    o_ref[0] = acc / l


def _mla_prompt(mq6, mk6, mv6, b, s, tq=256):
    tq = min(tq, KEY_TILE, s)
    nq = s // tq
    return pl.pallas_call(
        functools.partial(_mla_prompt_kernel, tq=tq),
        grid=(b, MLA_HEADS, nq),
        in_specs=[pl.BlockSpec((1, tq, MLA_QK), lambda bi, hi, ci: (hi, bi * nq + ci, 0)),
                  pl.BlockSpec((1, s, MLA_QK), lambda bi, hi, ci: (hi, bi, 0)),
                  pl.BlockSpec((1, s, MLA_V), lambda bi, hi, ci: (hi, bi, 0))],
        out_specs=pl.BlockSpec((1, tq, MLA_V), lambda bi, hi, ci: (hi, bi * nq + ci, 0)),
        out_shape=jax.ShapeDtypeStruct((MLA_HEADS, b * s, MLA_V), F32),
        compiler_params=_params("parallel", "parallel", "arbitrary"),
        name="mla_prompt",
    )(mq6, mk6, mv6)


def _merge_ffn_kernel(*refs, tm, seq_rows, tiles_per_seq, n_chunks, final):
    (x_ref, oa_ref, ob_ref, oc_ref, gout_ref, wout_ref, g2_ref, wg_ref, wu_ref, cw_ref, cb_ref, wd_ref, gf_ref) = refs[:13]
    carry_mode = seq_rows is None
    if carry_mode:
        y_ref, gl_ref, gtail_ref = refs[13:]
    else:
        st1_ref, st2_ref, y_ref, gl_ref = refs[13:]
    i = pl.program_id(0)
    if carry_mode:
        @pl.when(i == 0)
        def _init():
            gtail_ref[...] = jnp.zeros(gtail_ref.shape, F32)

    def group(ref, nh):
        parts = [ref[k] for k in range(nh)]
        ssq = parts[0] * parts[0]
        for p in parts[1:]:
            ssq = ssq + p * p
        r = lax.rsqrt(jnp.sum(ssq, axis=-1, keepdims=True) * (1.0 / (nh * HEAD_DIM)) + RMS_EPS)
        return [p * r for p in parts]

    o = jnp.concatenate(group(oa_ref, MOBA_HEADS) + group(ob_ref, SB_HEADS) + group(oc_ref, MLA_HEADS), axis=-1)
    x1 = x_ref[...] + _dot((o * gout_ref[...]).astype(BF16), wout_ref[...])
    h2 = _rms(x1, g2_ref[...]).astype(BF16)
    d_ff = wg_ref.shape[1]
    fc = d_ff // n_chunks
    row = lax.broadcasted_iota(jnp.int32, (tm, 1), 0)
    y = x1
    for ci in range(n_chunks):
        sl = slice(ci * fc, (ci + 1) * fc)
        g = _dot(h2, wg_ref[:, sl])
        u = _dot(h2, wu_ref[:, sl])
        if carry_mode:
            prev = jnp.where(i % tiles_per_seq == 0, 0.0, gtail_ref[:, sl])
            p6, p7 = prev[6:7, :], prev[7:8, :]
            hist1 = jnp.where(row == 0, p7, pltpu.roll(g, 1, 0))
            hist2 = jnp.where(row == 0, p6, jnp.where(row == 1, p7, pltpu.roll(g, 2, 0)))
            gtail_ref[:, sl] = g[tm - 8:, :]
            gl_ref[:, sl] = g[tm - 8:, :]
        else:
            rpos = row % seq_rows
            hist1 = jnp.where(rpos == 0, st1_ref[:, sl], pltpu.roll(g, 1, 0))
            hist2 = jnp.where(rpos < 2, st2_ref[:, sl], pltpu.roll(g, 2, 0))
            gl_ref[:, sl] = g
        cw = cw_ref[:, sl]
        conv = cb_ref[:, sl] + hist2 * cw[0:1, :]
        conv = conv + hist1 * cw[1:2, :]
        conv = conv + g * cw[2:3, :]
        act = conv * (1.0 / (1.0 + jnp.exp(-conv))) * u
        y = y + _dot(act.astype(BF16), wd_ref[sl, :])
    if final:
        y = _rms(y, gf_ref[...])
    y_ref[...] = y


def _merge_ffn(x2d, oa6, ob4, oc6, w, final, tm, seq_rows=None, tiles_per_seq=1, st=None):
    t, d = x2d.shape
    d_ff = w['wg'].shape[1]
    n_chunks = 2
    nt = t // tm
    carry_mode = seq_rows is None
    rows = lambda width: pl.BlockSpec((tm, width), lambda i: (i, 0))
    heads = lambda nh: pl.BlockSpec((nh, tm, HEAD_DIM), lambda i: (0, i, 0))
    ins = [x2d, oa6, ob4, oc6, w['g_out'], w['w_out'], w['g2'], w['wg'], w['wu'], w['cw'], w['cb'], w['wd'], w['g_final']]
    in_specs = [rows(d), heads(MOBA_HEADS), heads(SB_HEADS), heads(MLA_HEADS)]
    in_specs += [pl.BlockSpec(a.shape, lambda i, nd=a.ndim: (0,) * nd, pipeline_mode=pl.Buffered(1)) for a in ins[4:]]
    scratch = []
    if carry_mode:
        gl_shape, gl_spec = (nt * 8, d_ff), pl.BlockSpec((8, d_ff), lambda i: (i, 0))
        scratch = [pltpu.VMEM((8, d_ff), F32)]
    else:
        ins += list(st)
        in_specs += [rows(d_ff), rows(d_ff)]
        gl_shape, gl_spec = (t, d_ff), rows(d_ff)
    return pl.pallas_call(
        functools.partial(_merge_ffn_kernel, tm=tm, seq_rows=seq_rows, tiles_per_seq=tiles_per_seq,
                          n_chunks=n_chunks, final=final),
        grid=(nt,), in_specs=in_specs,
        out_specs=[rows(d), gl_spec],
        out_shape=[jax.ShapeDtypeStruct((t, d), F32), jax.ShapeDtypeStruct(gl_shape, F32)],
        scratch_shapes=scratch,
        compiler_params=_params("arbitrary"),
        name="merge_ffn_prompt" if carry_mode else "merge_ffn_sample",
    )(*ins)


def _page_specs(block, layer, n_pages, reverse):
    specs = []
    for i in range(PAGES_PER_STEP):
        if reverse:
            im = lambda n, s, pt, i=i: (layer, pt[n, n_pages - 1 - (s * PAGES_PER_STEP + i)], 0, 0)
        else:
            im = lambda n, s, pt, i=i: (layer, pt[n, s * PAGES_PER_STEP + i], 0, 0)
        specs.append(pl.BlockSpec((None, None) + block, im))
    return specs


def _seq_spec(shape):
    nd = len(shape)
    return pl.BlockSpec((None,) + shape, lambda n, s, pt: (n,) + (0,) * nd)


def _step_const_spec(shape):
    nd = len(shape)
    return pl.BlockSpec(shape, lambda n, s, pt: (0,) * nd)


def _sb_sample_kernel(pt_ref, q_ref, knt_ref, vnt_ref, tri_ref, *refs, nq):
    kp = refs[:PAGES_PER_STEP]
    vp = refs[PAGES_PER_STEP:2 * PAGES_PER_STEP]
    o_ref, r_ref, acc_ref = refs[2 * PAGES_PER_STEP:]
    s = pl.program_id(1)
    rows = SB_HEADS * nq
    q = q_ref[...]
    tri = tri_ref[...]

    @pl.when(s == 0)
    def _new_tokens():
        z = _dot(q, knt_ref[...].astype(BF16))
        qi = lax.broadcasted_iota(jnp.int32, (rows, PAGE_SIZE), 0) % nq
        ki = lax.broadcasted_iota(jnp.int32, (rows, PAGE_SIZE), 1)
        a, r_run = _sb_tile(z, ki < qi, tri[:PAGE_SIZE, :PAGE_SIZE], jnp.zeros((rows, 1), F32))
        r_ref[...] = r_run
        acc_ref[...] = _dot_t(a.astype(BF16), vnt_ref[...].astype(BF16))

    r_run = r_ref[...]
    acc = acc_ref[...]
    for i in range(0, PAGES_PER_STEP, 2):
        kt = jnp.concatenate([kp[i + 1][...], kp[i][...]], axis=1).astype(BF16)
        vt = jnp.concatenate([vp[i + 1][...], vp[i][...]], axis=1).astype(BF16)
        a, r_run = _sb_tile(_dot(q, kt), None, tri, r_run)
        acc = acc + _dot_t(a.astype(BF16), vt)
    r_ref[...] = r_run
    acc_ref[...] = acc

    @pl.when(s == pl.num_programs(1) - 1)
    def _out():
        o_ref[...] = acc


def _sb_sample(q, knt, vnt, cache_kt, cache_vt, page_table, layer, nq):
    n, n_pages = page_table.shape
    rows = SB_HEADS * nq
    page = (HEAD_DIM, PAGE_SIZE)
    grid_spec = pltpu.PrefetchScalarGridSpec(
        num_scalar_prefetch=1, grid=(n, n_pages // PAGES_PER_STEP),
        in_specs=[_seq_spec((rows, HEAD_DIM)), _seq_spec(page), _seq_spec(page), _step_const_spec((KEY_TILE, KEY_TILE))]
        + _page_specs(page, layer, n_pages, True) + _page_specs(page, layer, n_pages, True),
        out_specs=_seq_spec((rows, HEAD_DIM)),
        scratch_shapes=[pltpu.VMEM((rows, 1), F32), pltpu.VMEM((rows, HEAD_DIM), F32)])
    return pl.pallas_call(
        functools.partial(_sb_sample_kernel, nq=nq), grid_spec=grid_spec,
        out_shape=jax.ShapeDtypeStruct((n, rows, HEAD_DIM), F32),
        compiler_params=_params("parallel", "arbitrary"), name="sb_sample",
    )(page_table, q, knt, vnt, _tri(KEY_TILE), *([cache_kt] * PAGES_PER_STEP), *([cache_vt] * PAGES_PER_STEP))


def _moba_sample_kernel(pt_ref, q_ref, knt_ref, vnt_ref, slope_ref, *refs, nq, nb, past):
    kp = refs[:PAGES_PER_STEP]
    vp = refs[PAGES_PER_STEP:2 * PAGES_PER_STEP]
    o_ref, means_ref, mblk_ref, lblk_ref, oblk_ref = refs[2 * PAGES_PER_STEP:]
    s = pl.program_id(1)
    rows = MOBA_HEADS * nq
    width = MOBA_KV_HEADS * HEAD_DIM
    q = q_ref[...]
    qb = q.astype(BF16)
    slope = slope_ref[...]
    qpos = (past + lax.broadcasted_iota(jnp.int32, (rows, MOBA_BLOCK), 0) % nq).astype(F32)
    lane = lax.broadcasted_iota(jnp.int32, (rows, LANES), 1)
    mlane = lax.broadcasted_iota(jnp.int32, (width, LANES), 1)

    @pl.when(s == 0)
    def _init():
        means_ref[...] = jnp.zeros(means_ref.shape, F32)
        mblk_ref[...] = jnp.zeros(mblk_ref.shape, F32)
        lblk_ref[...] = jnp.zeros(lblk_ref.shape, F32)

    blocks_per_step = PAGES_PER_STEP * PAGE_SIZE // MOBA_BLOCK
    for bi in range(blocks_per_step):
        j = s * blocks_per_step + bi
        kt = jnp.concatenate([kp[2 * bi][...], kp[2 * bi + 1][...]], axis=1)
        vt = jnp.concatenate([vp[2 * bi][...], vp[2 * bi + 1][...]], axis=1).astype(BF16)
        mean = jnp.sum(kt, axis=1, keepdims=True) * (1.0 / MOBA_BLOCK)
        means_ref[...] = jnp.where(mlane == j, mean, means_ref[...])
        kpos = (j * MOBA_BLOCK + lax.broadcasted_iota(jnp.int32, (rows, MOBA_BLOCK), 1)).astype(F32)
        sc = _dot(qb, kt.astype(BF16)) - slope[:, :1] * (qpos - kpos)
        m = jnp.max(sc, axis=1, keepdims=True)
        p = jnp.exp(sc - m)
        mblk_ref[...] = jnp.where(lane == j, m, mblk_ref[...])
        lblk_ref[...] = jnp.where(lane == j, jnp.sum(p, axis=1, keepdims=True), lblk_ref[...])
        oblk_ref[j] = _dot_t(p.astype(BF16), vt)

    @pl.when(s == pl.num_programs(1) - 1)
    def _combine():
        gate = _dot(q, means_ref[...], precision=HIGHEST)
        sel = _topk_mask(gate, lane < nb, lane.astype(F32), min(MOBA_TOPK, nb)) > 0.5
        qi = lax.broadcasted_iota(jnp.int32, (rows, PAGE_SIZE), 0) % nq
        ki = lax.broadcasted_iota(jnp.int32, (rows, PAGE_SIZE), 1)
        d_own = (qi - ki).astype(F32)
        s_own = _dot(qb, knt_ref[...].astype(BF16)) - slope * d_own
        s_own = jnp.where(jnp.logical_and(qi >= ki, ki < nq), s_own, -jnp.inf)
        m_own = jnp.max(s_own, axis=1, keepdims=True)
        p_own = jnp.exp(s_own - m_own)
        l_own = jnp.sum(p_own, axis=1, keepdims=True)
        o_own = _dot_t(p_own.astype(BF16), vnt_ref[...].astype(BF16))
        mblk = mblk_ref[...]
        m_all = jnp.maximum(jnp.max(jnp.where(sel, mblk, -jnp.inf), axis=1, keepdims=True), m_own)
        w = jnp.where(sel, jnp.exp(mblk - m_all), 0.0)
        w_own = jnp.exp(m_own - m_all)
        l_all = jnp.sum(w * lblk_ref[...], axis=1, keepdims=True) + w_own * l_own
        o_all = w_own * o_own
        for jj in range(nb):
            o_all = o_all + w[:, jj:jj + 1] * oblk_ref[jj]
        o_all = o_all / l_all
        first_kv = lax.broadcasted_iota(jnp.int32, (rows, HEAD_DIM), 0) < MOBA_GROUP * nq
        o_ref[...] = jnp.where(first_kv, o_all[:, :HEAD_DIM], o_all[:, HEAD_DIM:])


def _moba_sample(qx, knt, vnt, cache_kt, cache_vt, page_table, layer, nq):
    n, n_pages = page_table.shape
    past = n_pages * PAGE_SIZE
    assert past % MOBA_BLOCK == 0 and PAGES_PER_STEP % 2 == 0
    nb = past // MOBA_BLOCK
    assert nb <= LANES
    rows = MOBA_HEADS * nq
    width = MOBA_KV_HEADS * HEAD_DIM
    i = jnp.arange(1, MOBA_HEADS + 1, dtype=F32)
    slope_rows = jnp.broadcast_to(jnp.repeat(jnp.exp2(-8.0 * i / MOBA_HEADS), nq)[:, None], (rows, LANES))
    page = (width, PAGE_SIZE)
    grid_spec = pltpu.PrefetchScalarGridSpec(
        num_scalar_prefetch=1, grid=(n, n_pages // PAGES_PER_STEP),
        in_specs=[_seq_spec((rows, width)), _seq_spec(page), _seq_spec(page), _step_const_spec((rows, LANES))]
        + _page_specs(page, layer, n_pages, False) + _page_specs(page, layer, n_pages, False),
        out_specs=_seq_spec((rows, HEAD_DIM)),
        scratch_shapes=[pltpu.VMEM((width, LANES), F32), pltpu.VMEM((rows, LANES), F32), pltpu.VMEM((rows, LANES), F32),
                        pltpu.VMEM((nb, rows, width), F32)])
    return pl.pallas_call(
        functools.partial(_moba_sample_kernel, nq=nq, nb=nb, past=past), grid_spec=grid_spec,
        out_shape=jax.ShapeDtypeStruct((n, rows, HEAD_DIM), F32),
        compiler_params=_params("parallel", "arbitrary"), name="moba_sample",
    )(page_table, qx, knt, vnt, slope_rows, *([cache_kt] * PAGES_PER_STEP), *([cache_vt] * PAGES_PER_STEP))


def _mla_sample_kernel(pt_ref, ql_ref, qp_ref, cn_ref, knt_ref, wuv_ref, *refs, nq):
    lp = refs[:PAGES_PER_STEP]
    rp = refs[PAGES_PER_STEP:2 * PAGES_PER_STEP]
    o_ref, m_ref, l_ref, acc_ref = refs[2 * PAGES_PER_STEP:]
    s = pl.program_id(1)
    rows = MLA_HEADS * nq
    ql = ql_ref[...].astype(BF16)
    qp = qp_ref[...].astype(BF16)

    @pl.when(s == 0)
    def _init():
        m_ref[...] = jnp.full(m_ref.shape, -jnp.inf, F32)
        l_ref[...] = jnp.zeros(l_ref.shape, F32)
        acc_ref[...] = jnp.zeros(acc_ref.shape, F32)

    def update(sc, lat_b):
        m = m_ref[...]
        m_new = jnp.maximum(m, jnp.max(sc, axis=1, keepdims=True))
        p = jnp.exp(sc - m_new)
        alpha = jnp.exp(m - m_new)
        l_ref[...] = alpha * l_ref[...] + jnp.sum(p, axis=1, keepdims=True)
        acc_ref[...] = alpha * acc_ref[...] + _dot(p.astype(BF16), lat_b)
        m_ref[...] = m_new

    lat = jnp.concatenate([r[...] for r in lp], axis=0).astype(BF16)
    krt = jnp.concatenate([r[...] for r in rp], axis=1).astype(BF16)
    update(_dot_t(ql, lat) + _dot(qp, krt), lat)

    @pl.when(s == pl.num_programs(1) - 1)
    def _finish():
        cn = cn_ref[...].astype(BF16)
        sc = _dot_t(ql, cn) + _dot(qp, knt_ref[...].astype(BF16))
        qi = lax.broadcasted_iota(jnp.int32, (rows, PAGE_SIZE), 0) % nq
        ki = lax.broadcasted_iota(jnp.int32, (rows, PAGE_SIZE), 1)
        update(jnp.where(jnp.logical_and(ki <= qi, ki < nq), sc, -jnp.inf), cn)
        o_lat = (acc_ref[...] / l_ref[...]).astype(BF16)
        full = _dot(o_lat, wuv_ref[...])
        head = lax.broadcasted_iota(jnp.int32, (rows, MLA_V), 0) // nq
        out = jnp.zeros((rows, MLA_V), F32)
        for hh in range(MLA_HEADS):
            out = jnp.where(head == hh, full[:, hh * MLA_V:(hh + 1) * MLA_V], out)
        o_ref[...] = out


def _mla_sample(ql, qp, cn, knt, wuv, cache_lat, cache_krt, page_table, layer, nq):
    n, n_pages = page_table.shape
    rows = MLA_HEADS * nq
    lat_page = (PAGE_SIZE, MLA_KV_LORA)
    kr_page = (MLA_ROPE, PAGE_SIZE)
    grid_spec = pltpu.PrefetchScalarGridSpec(
        num_scalar_prefetch=1, grid=(n, n_pages // PAGES_PER_STEP),
        in_specs=[_seq_spec((rows, MLA_KV_LORA)), _seq_spec((rows, MLA_ROPE)), _seq_spec(lat_page), _seq_spec(kr_page),
                  _step_const_spec(wuv.shape)]
        + _page_specs(lat_page, layer, n_pages, False) + _page_specs(kr_page, layer, n_pages, False),
        out_specs=_seq_spec((rows, MLA_V)),
        scratch_shapes=[pltpu.VMEM((rows, 1), F32), pltpu.VMEM((rows, 1), F32), pltpu.VMEM((rows, MLA_KV_LORA), F32)])
    return pl.pallas_call(
        functools.partial(_mla_sample_kernel, nq=nq), grid_spec=grid_spec,
        out_shape=jax.ShapeDtypeStruct((n, rows, MLA_V), F32),
        compiler_params=_params("parallel", "arbitrary"), name="mla_sample",
    )(page_table, ql, qp, cn, knt, wuv, *([cache_lat] * PAGES_PER_STEP), *([cache_krt] * PAGES_PER_STEP))


def _layer_weights(l, p):
    w_in = p['w_in'][l]
    wa = w_in[:, :HI_COLS]
    wa_hi = wa.astype(BF16)
    half = MLA_ROPE // 2
    kpe0 = w_in.shape[1] - MLA_ROPE
    kpe_sw = jnp.concatenate([w_in[:, kpe0 + half:], w_in[:, kpe0:kpe0 + half]], axis=1)
    wq = p['mla_w_uq'][l].reshape(MLA_Q_LORA, MLA_HEADS, MLA_QK).transpose(1, 0, 2)
    wq_sw = jnp.concatenate([wq[..., :MLA_NOPE], wq[..., MLA_NOPE + half:], wq[..., MLA_NOPE:MLA_NOPE + half]], axis=-1)
    w_uk = p['mla_w_uk'][l].transpose(1, 0, 2)
    w_uv = p['mla_w_uv'][l]
    row = lambda a: a.reshape(1, -1)
    return dict(
        g1=row(p['norm1_g'][l]), wa_hi=wa_hi, wa_lo=(wa - wa_hi.astype(F32)).astype(BF16),
        wb=jnp.concatenate([w_in[:, HI_COLS:], kpe_sw], axis=1).astype(BF16),
        qn_g=row(p['mla_q_norm_g'][l]), wq6=wq.astype(BF16), wq6s=wq_sw.astype(BF16), kvn_g=row(p['mla_kv_norm_g'][l]),
        wk6=jnp.pad(w_uk, ((0, 0), (0, 0), (0, MLA_ROPE))).astype(BF16),
        ekr=jnp.concatenate([jnp.zeros((MLA_ROPE, MLA_NOPE), F32), jnp.eye(MLA_ROPE, dtype=F32)], axis=1).astype(BF16),
        wv6=w_uv.transpose(1, 0, 2).astype(BF16),
        wukt6=w_uk.transpose(0, 2, 1).astype(BF16),
        wuv_flat=w_uv.reshape(MLA_KV_LORA, MLA_HEADS * MLA_V).astype(BF16),
        g_out=row(p['out_norm_g'][l]), w_out=p['w_out'][l].astype(BF16), g2=row(p['norm2_g'][l]),
        wg=p['ffn_w_gate'][l].astype(BF16), wu=p['ffn_w_up'][l].astype(BF16), cw=p['ffn_conv_w'][l],
        cb=row(p['ffn_conv_b'][l]), wd=p['ffn_w_down'][l].astype(BF16), g_final=row(p['final_norm_g']))


def _rope_tables(pos):
    half = MLA_ROPE // 2
    inv = ROPE_THETA ** (-jnp.arange(half, dtype=F32) / half)
    ang = pos.astype(F32)[:, None] * inv[None, :]
    cos, sin = jnp.cos(ang), jnp.sin(ang)
    n = pos.shape[0]
    rc = jnp.concatenate([jnp.ones((n, MLA_NOPE), F32), cos, cos], axis=1)
    rs = jnp.concatenate([jnp.zeros((n, MLA_NOPE), F32), -sin, sin], axis=1)
    return rc, rs


def _prompt_forward(x, weights, depth):
    b, s, d = x.shape
    assert s % MOBA_BLOCK == 0
    t = b * s
    tm = min(256, s)
    rc, rs = _rope_tables(jnp.arange(s))
    rc, rs = jnp.tile(rc, (b, 1)), jnp.tile(rs, (b, 1))
    x2d = x.reshape(t, d)
    st = [[] for _ in range(7)]
    for l in range(depth):
        w = weights[l]
        (qa6, ka, ka2, va, va2, qb4, kb, vb, ckv, kpe, mq6, mk6, mv6) = _project(x2d, rc, rs, w, False, tm)
        oa6 = _moba_prompt(qa6, ka2, va2, b, s)
        ob4 = _sb_prompt(qb4, kb, vb, b, s)
        oc6 = _mla_prompt(mq6, mk6, mv6, b, s)
        x2d, gl = _merge_ffn(x2d, oa6, ob4, oc6, w, l == depth - 1, tm, tiles_per_seq=s // tm)
        buf = gl.reshape(b, s // tm, 8, -1)[:, -1, 8 - (CONV_W - 1):, :]
        for lst, a in zip(st, (ka.reshape(b, s, MOBA_KV_HEADS, HEAD_DIM), va.reshape(b, s, MOBA_KV_HEADS, HEAD_DIM),
                               kb.reshape(b, s, HEAD_DIM), vb.reshape(b, s, HEAD_DIM),
                               ckv.reshape(b, s, MLA_KV_LORA), kpe.reshape(b, s, MLA_ROPE), buf)):
            lst.append(a)
    return x2d.reshape(b, s, d), [jnp.stack(a, axis=0) for a in st]


def _pad_tokens_t(a, n, nq):
    a = a.reshape(n, nq, -1).transpose(0, 2, 1)
    return jnp.pad(a, ((0, 0), (0, 0), (0, PAGE_SIZE - nq)))


def _rows_per_seq(a, n, nq):
    h = a.shape[0]
    return a.reshape(h, n, nq, -1).transpose(1, 0, 2, 3).reshape(n, h * nq, -1)


def _heads_major(a, n, nq, h):
    return a.reshape(n, h, nq, -1).transpose(1, 0, 2, 3).reshape(h, n * nq, -1)


def _sample_forward(x, caches, s_conv, page_table, weights, depth):
    n, nq, d = x.shape
    assert nq <= 8 and CONV_W - 1 <= nq
    c_mk, c_mv, c_sk, c_sv, c_mc, c_mr = caches
    n_pages = page_table.shape[1]
    assert n_pages % PAGES_PER_STEP == 0
    past = n_pages * PAGE_SIZE
    t = n * nq
    tm = min(256, t)
    rc, rs = _rope_tables(past + jnp.arange(nq))
    rc, rs = jnp.tile(rc, (n, 1)), jnp.tile(rs, (n, 1))
    width = MOBA_KV_HEADS * HEAD_DIM
    mk_t = c_mk.transpose(0, 1, 3, 4, 2).reshape(c_mk.shape[0], c_mk.shape[1], width, PAGE_SIZE)
    mv_t = c_mv.transpose(0, 1, 3, 4, 2).reshape(c_mv.shape[0], c_mv.shape[1], width, PAGE_SIZE)
    sk_t = c_sk.transpose(0, 1, 3, 2)
    sv_t = c_sv.transpose(0, 1, 3, 2)
    mr_t = c_mr.transpose(0, 1, 3, 2)
    x2d = x.reshape(t, d)
    st = [[] for _ in range(7)]
    for l in range(depth):
        w = weights[l]
        (qa6, ka, va, qb4, kb, vb, ckv, kpe, qlat6, qpe6) = _project(x2d, rc, rs, w, True, tm)
        qa = _rows_per_seq(qa6, n, nq)
        kvh = (jnp.arange(MOBA_HEADS * nq) // (MOBA_GROUP * nq))[None, :, None]
        qx = jnp.concatenate([jnp.where(kvh == 0, qa, 0.0), jnp.where(kvh == 1, qa, 0.0)], axis=-1)
        oa = _moba_sample(qx, _pad_tokens_t(ka, n, nq), _pad_tokens_t(va, n, nq), mk_t, mv_t, page_table, l, nq)
        ob = _sb_sample(_rows_per_seq(qb4, n, nq), _pad_tokens_t(kb, n, nq), _pad_tokens_t(vb, n, nq),
                        sk_t, sv_t, page_table, l, nq)
        cn = jnp.pad(ckv.reshape(n, nq, MLA_KV_LORA), ((0, 0), (0, PAGE_SIZE - nq), (0, 0)))
        oc = _mla_sample(_rows_per_seq(qlat6, n, nq), _rows_per_seq(qpe6, n, nq), cn, _pad_tokens_t(kpe, n, nq),
                         w['wuv_flat'], c_mc, mr_t, page_table, l, nq)
        buf = s_conv[l]
        zero = jnp.zeros((n, nq - 1, buf.shape[-1]), F32)
        st1 = jnp.concatenate([buf[:, 1:2], zero], axis=1).reshape(t, -1)
        st2 = jnp.concatenate([buf, zero[:, :nq - 2]], axis=1).reshape(t, -1)
        x2d, g = _merge_ffn(x2d, _heads_major(oa, n, nq, MOBA_HEADS), _heads_major(ob, n, nq, SB_HEADS),
                            _heads_major(oc, n, nq, MLA_HEADS), w, l == depth - 1, tm, seq_rows=nq, st=(st1, st2))
        new_buf = g.reshape(n, nq, -1)[:, nq - (CONV_W - 1):, :]
        for lst, a in zip(st, (ka.reshape(n, nq, MOBA_KV_HEADS, HEAD_DIM), va.reshape(n, nq, MOBA_KV_HEADS, HEAD_DIM),
                               kb.reshape(n, nq, HEAD_DIM), vb.reshape(n, nq, HEAD_DIM),
                               ckv.reshape(n, nq, MLA_KV_LORA), kpe.reshape(n, nq, MLA_ROPE), new_buf)):
            lst.append(a)
    return x2d.reshape(n, nq, d), [jnp.stack(a, axis=0) for a in st]


def kernel(x_prompt, x_sample, cache_moba_k, cache_moba_v, cache_sb_k, cache_sb_v, cache_mla_latent, cache_mla_krope, state_ffn_conv, page_table, norm1_g, w_in, mla_q_norm_g, mla_w_uq, mla_kv_norm_g, mla_w_uk, mla_w_uv, out_norm_g, w_out, norm2_g, ffn_w_gate, ffn_w_up, ffn_conv_w, ffn_conv_b, ffn_w_down, final_norm_g):
    p = dict(norm1_g=norm1_g, w_in=w_in, mla_q_norm_g=mla_q_norm_g, mla_w_uq=mla_w_uq,
             mla_kv_norm_g=mla_kv_norm_g, mla_w_uk=mla_w_uk, mla_w_uv=mla_w_uv, out_norm_g=out_norm_g,
             w_out=w_out, norm2_g=norm2_g, ffn_w_gate=ffn_w_gate, ffn_w_up=ffn_w_up,
             ffn_conv_w=ffn_conv_w, ffn_conv_b=ffn_conv_b, ffn_w_down=ffn_w_down, final_norm_g=final_norm_g)
    depth = w_in.shape[0]
    weights = [_layer_weights(l, p) for l in range(depth)]
    y_prompt, pst = _prompt_forward(x_prompt, weights, depth)
    y_sample, sst = _sample_forward(
        x_sample, (cache_moba_k, cache_moba_v, cache_sb_k, cache_sb_v, cache_mla_latent, cache_mla_krope),
        state_ffn_conv, page_table, weights, depth)
    return (y_prompt, y_sample, *pst, *sst)
```

```python
import functools

import jax
import jax.numpy as jnp
from jax import lax
from jax.experimental import pallas as pl
from jax.experimental.pallas import tpu as pltpu

F32 = jnp.float32
BF16 = jnp.bfloat16
HIGHEST = lax.Precision.HIGHEST

HEAD_DIM = 64
MOBA_HEADS = 6
MOBA_KV_HEADS = 2
MOBA_GROUP = MOBA_HEADS // MOBA_KV_HEADS
MOBA_BLOCK = 256
MOBA_TOPK = 3
SB_HEADS = 4
MLA_HEADS = 6
MLA_Q_LORA = 256
MLA_KV_LORA = 256
MLA_NOPE = 64
MLA_ROPE = 32
MLA_QK = MLA_NOPE + MLA_ROPE
MLA_V = 64
ROPE_THETA = 10000.0
MIX_A = MOBA_HEADS * HEAD_DIM
MIX_B = SB_HEADS * HEAD_DIM
MIX_C = MLA_HEADS * MLA_V
CONV_W = 3
RMS_EPS = 1e-6
PAGE_SIZE = 128

V7X_VMEM_BYTES = 64 * 1024 * 1024
VMEM_LIMIT = (V7X_VMEM_BYTES * 7) // 8
LANES = 128

KEY_TILE = 256
MASK_NEG = -1e30
PAGES_PER_STEP = 16
HI_COLS = MIX_A + MOBA_KV_HEADS * HEAD_DIM


def _dot(a, b, precision=None):
    return jnp.dot(a, b, preferred_element_type=F32, precision=precision)


def _dot_t(a, b, precision=None):
    return lax.dot_general(a, b, (((1,), (1,)), ((), ())), preferred_element_type=F32, precision=precision)


def _rms(x, g):
    return x * lax.rsqrt(jnp.mean(x * x, axis=-1, keepdims=True) + RMS_EPS) * g


def _params(*sem):
    return pltpu.CompilerParams(dimension_semantics=sem, vmem_limit_bytes=VMEM_LIMIT)


def _const_spec(shape):
    nd = len(shape)
    return pl.BlockSpec(shape, lambda *_: (0,) * nd)


def _proj_kernel(*refs, absorbed):
    (x_ref, g1_ref, wah_ref, wal_ref, wb_ref, qng_ref, wq_ref, wqs_ref, kvg_ref, rc_ref, rs_ref) = refs[:11]
    if absorbed:
        (wukt_ref, qa_ref, ka_ref, va_ref, qb_ref, kb_ref, vb_ref, ckv_ref, kpe_ref, qlat_ref, qpe_ref) = refs[11:]
    else:
        (wk_ref, ekr_ref, wv_ref, qa_ref, ka_ref, ka2_ref, va_ref, va2_ref, qb_ref, kb_ref, vb_ref, ckv_ref, kpe_ref,
         mq_ref, mk_ref, mv_ref) = refs[11:]
    h = _rms(x_ref[...], g1_ref[...])
    h_hi = h.astype(BF16)
    h_lo = (h - h_hi.astype(F32)).astype(BF16)
    pa = _dot(h_hi, wah_ref[...]) + (_dot(h_lo, wah_ref[...]) + _dot(h_hi, wal_ref[...]))
    pb = _dot(h_hi, wb_ref[...])
    for k in range(MOBA_HEADS):
        qa_ref[k] = pa[:, k * HEAD_DIM:(k + 1) * HEAD_DIM] * (HEAD_DIM ** -0.5)
    ka = pa[:, MIX_A:MIX_A + 2 * HEAD_DIM]
    ka_ref[...] = ka
    va = pb[:, 0:128]
    va_ref[...] = va
    if not absorbed:
        for k in range(MOBA_KV_HEADS):
            ka2_ref[k] = ka[:, k * HEAD_DIM:(k + 1) * HEAD_DIM]
            va2_ref[k] = va[:, k * HEAD_DIM:(k + 1) * HEAD_DIM].astype(BF16)
    for k in range(SB_HEADS):
        qb_ref[k] = (pb[:, 128 + k * HEAD_DIM:128 + (k + 1) * HEAD_DIM] * (HEAD_DIM ** -0.5)).astype(BF16)
    kb_ref[...] = pb[:, 384:448]
    vb_ref[...] = pb[:, 448:512]
    cqn = _rms(pb[:, 512:768], qng_ref[...]).astype(BF16)
    ckvn = _rms(pb[:, 768:1024], kvg_ref[...])
    ckv_ref[...] = ckvn
    rc = rc_ref[...]
    rs = rs_ref[...]
    kr = pb[:, 1024:1056] * rc[:, MLA_NOPE:] + pb[:, 1056:1088] * rs[:, MLA_NOPE:]
    kpe_ref[...] = kr
    scale = MLA_QK ** -0.5
    if not absorbed:
        ckvn_b = ckvn.astype(BF16)
        kr_b = kr.astype(BF16)
    for hh in range(MLA_HEADS):
        qh = (_dot(cqn, wq_ref[hh]) * rc + _dot(cqn, wqs_ref[hh]) * rs) * scale
        if absorbed:
            qlat_ref[hh] = _dot(qh[:, :MLA_NOPE].astype(BF16), wukt_ref[hh])
            qpe_ref[hh] = qh[:, MLA_NOPE:]
        else:
            mq_ref[hh] = qh.astype(BF16)
            mk_ref[hh] = (_dot(ckvn_b, wk_ref[hh]) + _dot(kr_b, ekr_ref[...])).astype(BF16)
            mv_ref[hh] = _dot(ckvn_b, wv_ref[hh]).astype(BF16)


def _project(x2d, rc, rs, w, absorbed, tm):
    t, d = x2d.shape
    grid = (t // tm,)
    rows = lambda width: pl.BlockSpec((tm, width), lambda i: (i, 0))
    heads = lambda nh, width: pl.BlockSpec((nh, tm, width), lambda i: (0, i, 0))
    ins = [x2d, w['g1'], w['wa_hi'], w['wa_lo'], w['wb'], w['qn_g'], w['wq6'], w['wq6s'], w['kvn_g'], rc, rs]
    in_specs = [rows(d)] + [_const_spec(a.shape) for a in ins[1:9]] + [rows(MLA_QK), rows(MLA_QK)]
    if absorbed:
        extra = [w['wukt6']]
        outs = [((MOBA_HEADS, t, HEAD_DIM), F32, heads(MOBA_HEADS, HEAD_DIM)),
                ((t, 128), F32, rows(128)), ((t, 128), F32, rows(128)),
                ((SB_HEADS, t, HEAD_DIM), BF16, heads(SB_HEADS, HEAD_DIM)),
                ((t, HEAD_DIM), F32, rows(HEAD_DIM)), ((t, HEAD_DIM), F32, rows(HEAD_DIM)),
                ((t, MLA_KV_LORA), F32, rows(MLA_KV_LORA)), ((t, MLA_ROPE), F32, rows(MLA_ROPE)),
                ((MLA_HEADS, t, MLA_KV_LORA), F32, heads(MLA_HEADS, MLA_KV_LORA)),
                ((MLA_HEADS, t, MLA_ROPE), F32, heads(MLA_HEADS, MLA_ROPE))]
    else:
        extra = [w['wk6'], w['ekr'], w['wv6']]
        outs = [((MOBA_HEADS, t, HEAD_DIM), F32, heads(MOBA_HEADS, HEAD_DIM)),
                ((t, 128), F32, rows(128)), ((MOBA_KV_HEADS, t, HEAD_DIM), F32, heads(MOBA_KV_HEADS, HEAD_DIM)),
                ((t, 128), F32, rows(128)), ((MOBA_KV_HEADS, t, HEAD_DIM), BF16, heads(MOBA_KV_HEADS, HEAD_DIM)),
                ((SB_HEADS, t, HEAD_DIM), BF16, heads(SB_HEADS, HEAD_DIM)),
                ((t, HEAD_DIM), F32, rows(HEAD_DIM)), ((t, HEAD_DIM), F32, rows(HEAD_DIM)),
                ((t, MLA_KV_LORA), F32, rows(MLA_KV_LORA)), ((t, MLA_ROPE), F32, rows(MLA_ROPE)),
                ((MLA_HEADS, t, MLA_QK), BF16, heads(MLA_HEADS, MLA_QK)),
                ((MLA_HEADS, t, MLA_QK), BF16, heads(MLA_HEADS, MLA_QK)),
                ((MLA_HEADS, t, MLA_V), BF16, heads(MLA_HEADS, MLA_V))]
    ins += extra
    in_specs += [_const_spec(a.shape) for a in extra]
    return pl.pallas_call(
        functools.partial(_proj_kernel, absorbed=absorbed),
        grid=grid, in_specs=in_specs,
        out_specs=[o[2] for o in outs],
        out_shape=[jax.ShapeDtypeStruct(o[0], o[1]) for o in outs],
        compiler_params=_params("parallel"),
        name="proj_sample" if absorbed else "proj_prompt",
    )(*ins)


def _topk_mask(gate, allowed, col_f, k_sel):
    g = jnp.where(allowed, gate, -jnp.inf)
    sel = jnp.zeros(gate.shape, F32)
    for _ in range(k_sel):
        m = jnp.max(g, axis=1, keepdims=True)
        cand = jnp.where(g == m, col_f, 1e9)
        cand = jnp.where(m > -jnp.inf, cand, 1e9)
        first = jnp.min(cand, axis=1, keepdims=True)
        pick = col_f == first
        sel = jnp.where(pick, 1.0, sel)
        g = jnp.where(pick, -jnp.inf, g)
    return sel


def _moba_prompt_kernel(q_ref, k_ref, v_ref, slope_ref, d0_ref, o_ref, means_ref, kext_ref, *, tq, nb, ext):
    c = pl.program_id(1)
    rows = MOBA_GROUP * tq
    kv_heads = range(MOBA_KV_HEADS)

    @pl.when(c == 0)
    def _build():
        means_ref[...] = jnp.zeros(means_ref.shape, F32)
        col = lax.broadcasted_iota(jnp.int32, (MOBA_BLOCK, ext), 1)

        def body(j, carry):
            r0 = pl.multiple_of(j * MOBA_BLOCK, MOBA_BLOCK)
            onehot = jnp.where(col == j, 1.0, 0.0)
            for hk in kv_heads:
                kb = k_ref[hk, pl.ds(r0, MOBA_BLOCK), :]
                means_ref[hk, pl.ds(j, 1), :] = jnp.sum(kb, axis=0, keepdims=True) * (1.0 / MOBA_BLOCK)
                kext_ref[hk, pl.ds(r0, MOBA_BLOCK), :] = jnp.concatenate([kb, onehot], axis=1).astype(BF16)
            return carry

        lax.fori_loop(0, nb, body, 0)

    n_past = (c * tq) // MOBA_BLOCK
    col_i = lax.broadcasted_iota(jnp.int32, (rows, ext), 1)
    q_ext = []
    for hk in kv_heads:
        q = q_ref[hk * MOBA_GROUP:(hk + 1) * MOBA_GROUP].reshape(rows, HEAD_DIM)
        gate = _dot_t(q, means_ref[hk], precision=HIGHEST)
        sel = _topk_mask(gate, col_i < n_past, col_i.astype(F32), min(MOBA_TOPK, nb))
        keep = jnp.logical_or(sel > 0.5, col_i == n_past)
        q_ext.append(jnp.concatenate([q, jnp.where(keep, 0.0, MASK_NEG)], axis=1).astype(BF16))
    d0 = d0_ref[...]

    def block(j, carry, causal):
        r0 = pl.multiple_of(j * MOBA_BLOCK, MOBA_BLOCK)
        d = d0 + (c * tq - j * MOBA_BLOCK).astype(F32)
        out = []
        for hk in kv_heads:
            m, l, acc = carry[hk]
            s = _dot_t(q_ext[hk], kext_ref[hk, pl.ds(r0, MOBA_BLOCK), :]) - slope_ref[hk] * d
            if causal:
                s = jnp.where(d >= 0, s, -jnp.inf)
            m_new = jnp.maximum(m, jnp.max(s, axis=1, keepdims=True))
            p = jnp.exp(s - m_new)
            alpha = jnp.exp(m - m_new)
            l = alpha * l + jnp.sum(p, axis=1, keepdims=True)
            acc = alpha * acc + _dot(p.astype(BF16), v_ref[hk, pl.ds(r0, MOBA_BLOCK), :])
            out.append((m_new, l, acc))
        return tuple(out)

    init = tuple((jnp.full((rows, 1), -jnp.inf, F32), jnp.zeros((rows, 1), F32), jnp.zeros((rows, HEAD_DIM), F32))
                 for _ in kv_heads)
    carry = lax.fori_loop(0, n_past, lambda j, cr: block(j, cr, False), init)
    carry = block(n_past, carry, True)
    for hk in kv_heads:
        _, l, acc = carry[hk]
        o_ref[hk * MOBA_GROUP:(hk + 1) * MOBA_GROUP] = (acc / l).reshape(MOBA_GROUP, tq, HEAD_DIM)


def _moba_prompt(qa6, ka2, va2, b, s, tq=128):
    nq = s // tq
    nb = s // MOBA_BLOCK
    ext = max(32, -(-nb // 32) * 32)
    rows = MOBA_GROUP * tq
    i = jnp.arange(1, MOBA_HEADS + 1, dtype=F32)
    slopes = jnp.exp2(-8.0 * i / MOBA_HEADS).reshape(MOBA_KV_HEADS, MOBA_GROUP)
    slope_full = jnp.broadcast_to(jnp.repeat(slopes, tq, axis=1)[:, :, None], (MOBA_KV_HEADS, rows, MOBA_BLOCK))
    qi = jnp.tile(jnp.arange(tq, dtype=F32), MOBA_GROUP)
    d0 = qi[:, None] - jnp.arange(MOBA_BLOCK, dtype=F32)[None, :]
    once = pl.Buffered(1)
    return pl.pallas_call(
        functools.partial(_moba_prompt_kernel, tq=tq, nb=nb, ext=ext),
        grid=(b, nq),
        in_specs=[pl.BlockSpec((MOBA_HEADS, tq, HEAD_DIM), lambda bi, ci: (0, bi * nq + ci, 0)),
                  pl.BlockSpec((MOBA_KV_HEADS, s, HEAD_DIM), lambda bi, ci: (0, bi, 0), pipeline_mode=once),
                  pl.BlockSpec((MOBA_KV_HEADS, s, HEAD_DIM), lambda bi, ci: (0, bi, 0), pipeline_mode=once),
                  _const_spec((MOBA_KV_HEADS, rows, MOBA_BLOCK)),
                  _const_spec((rows, MOBA_BLOCK))],
        out_specs=pl.BlockSpec((MOBA_HEADS, tq, HEAD_DIM), lambda bi, ci: (0, bi * nq + ci, 0)),
        out_shape=jax.ShapeDtypeStruct((MOBA_HEADS, b * s, HEAD_DIM), F32),
        scratch_shapes=[pltpu.VMEM((MOBA_KV_HEADS, ext, HEAD_DIM), F32),
                        pltpu.VMEM((MOBA_KV_HEADS, s, HEAD_DIM + ext), BF16)],
        compiler_params=_params("parallel", "arbitrary"),
        name="moba_prompt",
    )(qa6, ka2, va2, slope_full, d0)


def _sb_tile(z, mask, neg_tri, r_run):
    sp = jnp.maximum(z, 0.0) + jnp.log(1.0 + jnp.exp(-jnp.abs(z)))
    if mask is not None:
        sp = jnp.where(mask, sp, 0.0)
    tail = _dot(sp.astype(BF16), neg_tri)
    a = jnp.exp((z - sp) + tail + r_run)
    if mask is not None:
        a = jnp.where(mask, a, 0.0)
    return a, r_run - jnp.sum(sp, axis=1, keepdims=True)


def _sb_prompt_kernel(q_ref, k_ref, v_ref, tri_ref, o_ref, kb_ref, vb_ref, *, tq, n_chains):
    c = pl.program_id(1)

    @pl.when(c == 0)
    def _cast():
        kb_ref[...] = k_ref[...].astype(BF16)
        vb_ref[...] = v_ref[...].astype(BF16)

    hpc = SB_HEADS // n_chains
    rows = hpc * tq
    qs = [q_ref[ch * hpc:(ch + 1) * hpc].reshape(rows, HEAD_DIM) for ch in range(n_chains)]
    tri = tri_ref[...]
    jd = (c * tq) // KEY_TILE

    def tile(j, carry, masked):
        r0 = pl.multiple_of(j * KEY_TILE, KEY_TILE)
        kt = kb_ref[pl.ds(r0, KEY_TILE), :]
        vt = vb_ref[pl.ds(r0, KEY_TILE), :]
        mask = None
        if masked:
            qpos = c * tq + (lax.broadcasted_iota(jnp.int32, (rows, KEY_TILE), 0) & (tq - 1))
            kpos = j * KEY_TILE + lax.broadcasted_iota(jnp.int32, (rows, KEY_TILE), 1)
            mask = kpos < qpos
        out = []
        for ch in range(n_chains):
            r_run, acc = carry[ch]
            a, r_run = _sb_tile(_dot_t(qs[ch], kt), mask, tri, r_run)
            out.append((r_run, acc + _dot(a.astype(BF16), vt)))
        return tuple(out)

    init = tuple((jnp.zeros((rows, 1), F32), jnp.zeros((rows, HEAD_DIM), F32)) for _ in range(n_chains))
    carry = tile(jd, init, True)
    carry = lax.fori_loop(0, jd, lambda i, cr: tile(jd - 1 - i, cr, False), carry)
    for ch in range(n_chains):
        o_ref[ch * hpc:(ch + 1) * hpc] = carry[ch][1].reshape(hpc, tq, HEAD_DIM)


def _neg_tri(n):
    r = jnp.arange(n)
    return -((r[:, None] > r[None, :]).astype(BF16))


def _sb_prompt(qb4, kb, vb, b, s, tq=128, n_chains=1):
    nq = s // tq
    return pl.pallas_call(
        functools.partial(_sb_prompt_kernel, tq=tq, n_chains=n_chains),
        grid=(b, nq),
        in_specs=[pl.BlockSpec((SB_HEADS, tq, HEAD_DIM), lambda bi, ci: (0, bi * nq + ci, 0)),
                  pl.BlockSpec((s, HEAD_DIM), lambda bi, ci: (bi, 0)),
                  pl.BlockSpec((s, HEAD_DIM), lambda bi, ci: (bi, 0)),
                  _const_spec((KEY_TILE, KEY_TILE))],
        out_specs=pl.BlockSpec((SB_HEADS, tq, HEAD_DIM), lambda bi, ci: (0, bi * nq + ci, 0)),
        out_shape=jax.ShapeDtypeStruct((SB_HEADS, b * s, HEAD_DIM), F32),
        scratch_shapes=[pltpu.VMEM((s, HEAD_DIM), BF16), pltpu.VMEM((s, HEAD_DIM), BF16)],
        compiler_params=_params("parallel", "arbitrary"),
        name="sb_prompt",
    )(qb4, kb, vb, _neg_tri(KEY_TILE))


def _mla_prompt_kernel(q_ref, k_ref, v_ref, o_ref, *, tq, nh):
    c = pl.program_id(2)
    qs = [q_ref[h] for h in range(nh)]
    jd = (c * tq) // KEY_TILE

    def tile(j, carry, masked):
        r0 = pl.multiple_of(j * KEY_TILE, KEY_TILE)
        if masked:
            qpos = c * tq + lax.broadcasted_iota(jnp.int32, (tq, KEY_TILE), 0)
            kpos = j * KEY_TILE + lax.broadcasted_iota(jnp.int32, (tq, KEY_TILE), 1)
            visible = kpos <= qpos
        out = []
        for h in range(nh):
            m, l, acc = carry[h]
            s = _dot_t(qs[h], k_ref[h, pl.ds(r0, KEY_TILE), :])
            if masked:
                s = jnp.where(visible, s, -jnp.inf)
            m_new = jnp.maximum(m, jnp.max(s, axis=1, keepdims=True))
            p = jnp.exp(s - m_new)
            alpha = jnp.exp(m - m_new)
            l = alpha * l + jnp.sum(p, axis=1, keepdims=True)
            acc = alpha * acc + _dot(p.astype(BF16), v_ref[h, pl.ds(r0, KEY_TILE), :])
            out.append((m_new, l, acc))
        return tuple(out)

    init = tuple((jnp.full((tq, 1), -jnp.inf, F32), jnp.zeros((tq, 1), F32), jnp.zeros((tq, MLA_V), F32))
                 for _ in range(nh))
    carry = tile(jd, init, True)
    carry = lax.fori_loop(0, jd, lambda j, cr: tile(j, cr, False), carry)
    for h in range(nh):
        _, l, acc = carry[h]
        o_ref[h] = acc / l


def _mla_prompt(mq6, mk6, mv6, b, s, tq=256, nh=MLA_HEADS):
    tq = min(tq, KEY_TILE, s)
    nq = s // tq
    once = pl.Buffered(1)
    return pl.pallas_call(
        functools.partial(_mla_prompt_kernel, tq=tq, nh=nh),
        grid=(b, MLA_HEADS // nh, nq),
        in_specs=[pl.BlockSpec((nh, tq, MLA_QK), lambda bi, hi, ci: (hi, bi * nq + ci, 0)),
                  pl.BlockSpec((nh, s, MLA_QK), lambda bi, hi, ci: (hi, bi, 0), pipeline_mode=once),
                  pl.BlockSpec((nh, s, MLA_V), lambda bi, hi, ci: (hi, bi, 0), pipeline_mode=once)],
        out_specs=pl.BlockSpec((nh, tq, MLA_V), lambda bi, hi, ci: (hi, bi * nq + ci, 0)),
        out_shape=jax.ShapeDtypeStruct((MLA_HEADS, b * s, MLA_V), F32),
        compiler_params=_params("parallel", "parallel", "arbitrary"),
        name="mla_prompt",
    )(mq6, mk6, mv6)


def _merge_ffn_kernel(*refs, tm, seq_rows, tiles_per_seq, n_chunks, final):
    (x_ref, oa_ref, ob_ref, oc_ref, gout_ref, wout_ref, g2_ref, wg_ref, wu_ref, cw_ref, cb_ref, wd_ref, gf_ref) = refs[:13]
    carry_mode = seq_rows is None
    if carry_mode:
        y_ref, gl_ref, gtail_ref = refs[13:]
    else:
        st1_ref, st2_ref, y_ref, gl_ref = refs[13:]
    i = pl.program_id(0)
    if carry_mode:
        @pl.when(i == 0)
        def _init():
            gtail_ref[...] = jnp.zeros(gtail_ref.shape, F32)

    def group(ref, nh):
        parts = [ref[k] for k in range(nh)]
        ssq = parts[0] * parts[0]
        for p in parts[1:]:
            ssq = ssq + p * p
        r = lax.rsqrt(jnp.sum(ssq, axis=-1, keepdims=True) * (1.0 / (nh * HEAD_DIM)) + RMS_EPS)
        return [p * r for p in parts]

    o = jnp.concatenate(group(oa_ref, MOBA_HEADS) + group(ob_ref, SB_HEADS) + group(oc_ref, MLA_HEADS), axis=-1)
    x1 = x_ref[...] + _dot((o * gout_ref[...]).astype(BF16), wout_ref[...])
    h2 = _rms(x1, g2_ref[...]).astype(BF16)
    d_ff = wg_ref.shape[1]
    fc = d_ff // n_chunks
    row = lax.broadcasted_iota(jnp.int32, (tm, 1), 0)
    y = x1
    for ci in range(n_chunks):
        sl = slice(ci * fc, (ci + 1) * fc)
        g = _dot(h2, wg_ref[:, sl])
        u = _dot(h2, wu_ref[:, sl])
        if carry_mode:
            prev = jnp.where(i % tiles_per_seq == 0, 0.0, gtail_ref[:, sl])
            p6, p7 = prev[6:7, :], prev[7:8, :]
            hist1 = jnp.where(row == 0, p7, pltpu.roll(g, 1, 0))
            hist2 = jnp.where(row == 0, p6, jnp.where(row == 1, p7, pltpu.roll(g, 2, 0)))
            gtail_ref[:, sl] = g[tm - 8:, :]
            gl_ref[:, sl] = g[tm - 8:, :]
        else:
            rpos = row % seq_rows
            hist1 = jnp.where(rpos == 0, st1_ref[:, sl], pltpu.roll(g, 1, 0))
            hist2 = jnp.where(rpos < 2, st2_ref[:, sl], pltpu.roll(g, 2, 0))
            gl_ref[:, sl] = g
        cw = cw_ref[:, sl]
        conv = cb_ref[:, sl] + hist2 * cw[0:1, :]
        conv = conv + hist1 * cw[1:2, :]
        conv = conv + g * cw[2:3, :]
        act = conv * (1.0 / (1.0 + jnp.exp(-conv))) * u
        y = y + _dot(act.astype(BF16), wd_ref[sl, :])
    if final:
        y = _rms(y, gf_ref[...])
    y_ref[...] = y


def _merge_ffn(x2d, oa6, ob4, oc6, w, final, tm, seq_rows=None, tiles_per_seq=1, st=None):
    t, d = x2d.shape
    d_ff = w['wg'].shape[1]
    n_chunks = 2
    nt = t // tm
    carry_mode = seq_rows is None
    rows = lambda width: pl.BlockSpec((tm, width), lambda i: (i, 0))
    heads = lambda nh: pl.BlockSpec((nh, tm, HEAD_DIM), lambda i: (0, i, 0))
    ins = [x2d, oa6, ob4, oc6, w['g_out'], w['w_out'], w['g2'], w['wg'], w['wu'], w['cw'], w['cb'], w['wd'], w['g_final']]
    in_specs = [rows(d), heads(MOBA_HEADS), heads(SB_HEADS), heads(MLA_HEADS)]
    in_specs += [pl.BlockSpec(a.shape, lambda i, nd=a.ndim: (0,) * nd, pipeline_mode=pl.Buffered(1)) for a in ins[4:]]
    scratch = []
    if carry_mode:
        gl_shape, gl_spec = (nt * 8, d_ff), pl.BlockSpec((8, d_ff), lambda i: (i, 0))
        scratch = [pltpu.VMEM((8, d_ff), F32)]
    else:
        ins += list(st)
        in_specs += [rows(d_ff), rows(d_ff)]
        gl_shape, gl_spec = (t, d_ff), rows(d_ff)
    return pl.pallas_call(
        functools.partial(_merge_ffn_kernel, tm=tm, seq_rows=seq_rows, tiles_per_seq=tiles_per_seq,
                          n_chunks=n_chunks, final=final),
        grid=(nt,), in_specs=in_specs,
        out_specs=[rows(d), gl_spec],
        out_shape=[jax.ShapeDtypeStruct((t, d), F32), jax.ShapeDtypeStruct(gl_shape, F32)],
        scratch_shapes=scratch,
        compiler_params=_params("arbitrary"),
        name="merge_ffn_prompt" if carry_mode else "merge_ffn_sample",
    )(*ins)


def _sample_attn_kernel(pt_ref,
                        qx_ref, mknt_ref, mvnt_ref, slope_ref,
                        sq_ref, sknt_ref, svnt_ref, ntri_ref,
                        ql_ref, qp_ref, cn_ref, krnt_ref, wuv_ref,
                        mk_hbm, mv_hbm, sk_hbm, sv_hbm, lat_hbm, kr_hbm,
                        oa_ref, ob_ref, oc_ref,
                        mk_buf, mv_buf, sk_buf, sv_buf, lat_buf, kr_buf, sems,
                        oblk_ref,
                        *, layer, nq, n_pages, nb, past):
    n = pl.program_id(0)
    n_seq = pl.num_programs(0)
    pps = PAGES_PER_STEP
    n_steps = n_pages // pps
    hbms = (mk_hbm, mv_hbm, sk_hbm, sv_hbm, lat_hbm, kr_hbm)
    bufs = (mk_buf, mv_buf, sk_buf, sv_buf, lat_buf, kr_buf)

    def page_copies(seq, step, slot):
        out = []
        for i in range(pps):
            page = pt_ref[seq, n_pages - 1 - (step * pps + i)]
            for a in range(len(hbms)):
                out.append(pltpu.make_async_copy(hbms[a].at[layer, page], bufs[a].at[slot, i], sems.at[a, slot]))
        return out

    @pl.when(n == 0)
    def _first_fetch():
        for cp in page_copies(0, 0, 0):
            cp.start()

    a_rows = MOBA_HEADS * nq
    b_rows = SB_HEADS * nq
    c_rows = MLA_HEADS * nq
    width = MOBA_KV_HEADS * HEAD_DIM
    qx = qx_ref[...]
    qxb = qx.astype(BF16)
    slope = slope_ref[...]
    sq = sq_ref[...]
    ql = ql_ref[...].astype(BF16)
    qp = qp_ref[...].astype(BF16)
    a_lane = lax.broadcasted_iota(jnp.int32, (a_rows, LANES), 1)
    m_lane = lax.broadcasted_iota(jnp.int32, (width, LANES), 1)

    qi_b = lax.broadcasted_iota(jnp.int32, (b_rows, PAGE_SIZE), 0) % nq
    ki_b = lax.broadcasted_iota(jnp.int32, (b_rows, PAGE_SIZE), 1)
    a_new, r0 = _sb_tile(_dot(sq, sknt_ref[...].astype(BF16)), ki_b < qi_b, ntri_ref[:PAGE_SIZE, :PAGE_SIZE],
                         jnp.zeros((b_rows, 1), F32))
    sb0 = _dot_t(a_new.astype(BF16), svnt_ref[...].astype(BF16))

    def softmax_update(state, sc, values):
        m, l, acc = state
        m_new = jnp.maximum(m, jnp.max(sc, axis=1, keepdims=True))
        p = jnp.exp(sc - m_new)
        alpha = jnp.exp(m - m_new)
        return m_new, alpha * l + jnp.sum(p, axis=1, keepdims=True), alpha * acc + _dot(p.astype(BF16), values)

    n_blk = pps * PAGE_SIZE // MOBA_BLOCK
    chunk = lambda x, c: x[:, c * MOBA_BLOCK:(c + 1) * MOBA_BLOCK]

    def step(s, carry):
        r_run, sb_acc, mla, means, mblk, lblk = carry
        g = n * n_steps + s
        slot = g % 2
        for cp in page_copies(n, s, slot):
            cp.wait()
        last_of_seq = s + 1 == n_steps
        nxt_seq = jnp.minimum(jnp.where(last_of_seq, n + 1, n), n_seq - 1)
        nxt_step = jnp.where(last_of_seq, 0, s + 1)

        @pl.when(g + 1 < n_seq * n_steps)
        def _prefetch():
            for cp in page_copies(nxt_seq, nxt_step, 1 - slot):
                cp.start()

        order = range(pps - 1, -1, -1)
        lanes_cat = lambda buf: jnp.concatenate([buf[slot, i] for i in order], axis=1)
        lat = lat_buf[slot].reshape(pps * PAGE_SIZE, MLA_KV_LORA).astype(BF16)
        krt = jnp.concatenate([kr_buf[slot, i] for i in range(pps)], axis=1).astype(BF16)
        skt = lanes_cat(sk_buf).astype(BF16)
        svt = lanes_cat(sv_buf).astype(BF16)
        mkt = lanes_cat(mk_buf)
        mvt = lanes_cat(mv_buf).astype(BF16)
        keys = pps * PAGE_SIZE
        blk0 = (n_pages - (s + 1) * pps) * PAGE_SIZE // MOBA_BLOCK

        c_sc = _dot_t(ql, lat) + _dot(qp, krt)
        b_z = _dot(sq, skt)
        a_sc = _dot(qxb, mkt.astype(BF16))

        sp = jnp.maximum(b_z, 0.0) + jnp.log(1.0 + jnp.exp(-jnp.abs(b_z)))
        sp_b = sp.astype(BF16)
        ntri = ntri_ref[...]
        tails = [_dot(chunk(sp_b, c), ntri) for c in range(n_blk)]
        sums = [jnp.sum(chunk(sp, c), axis=1, keepdims=True) for c in range(n_blk)]
        a_parts = [None] * n_blk
        for c in range(n_blk - 1, -1, -1):
            a_parts[c] = jnp.exp((chunk(b_z, c) - chunk(sp, c)) + tails[c] + r_run)
            r_run = r_run - sums[c]
        sb_acc = sb_acc + _dot_t(jnp.concatenate(a_parts, axis=1).astype(BF16), svt)

        mla = softmax_update(mla, c_sc, lat)

        kpos = (blk0 * MOBA_BLOCK + lax.broadcasted_iota(jnp.int32, (a_rows, keys), 1)).astype(F32)
        qpos = (past + lax.broadcasted_iota(jnp.int32, (a_rows, keys), 0) % nq).astype(F32)
        a_sc = a_sc - slope[:, :1] * (qpos - kpos)
        for c in range(n_blk):
            j = blk0 + c
            sc = chunk(a_sc, c)
            m = jnp.max(sc, axis=1, keepdims=True)
            p = jnp.exp(sc - m)
            mblk = jnp.where(a_lane == j, m, mblk)
            lblk = jnp.where(a_lane == j, jnp.sum(p, axis=1, keepdims=True), lblk)
            means = jnp.where(m_lane == j, jnp.sum(chunk(mkt, c), axis=1, keepdims=True) * (1.0 / MOBA_BLOCK), means)
            oblk_ref[j] = _dot_t(p.astype(BF16), chunk(mvt, c))
        return r_run, sb_acc, mla, means, mblk, lblk

    mla0 = (jnp.full((c_rows, 1), -jnp.inf, F32), jnp.zeros((c_rows, 1), F32), jnp.zeros((c_rows, MLA_KV_LORA), F32))
    init = (r0, sb0, mla0, jnp.zeros((width, LANES), F32), jnp.zeros((a_rows, LANES), F32), jnp.zeros((a_rows, LANES), F32))
    _, sb_acc, mla, means, mblk, lblk = lax.fori_loop(0, n_steps, step, init)
    ob_ref[...] = sb_acc

    cn = cn_ref[...].astype(BF16)
    qi_c = lax.broadcasted_iota(jnp.int32, (c_rows, PAGE_SIZE), 0) % nq
    ki_c = lax.broadcasted_iota(jnp.int32, (c_rows, PAGE_SIZE), 1)
    sc = _dot_t(ql, cn) + _dot(qp, krnt_ref[...].astype(BF16))
    sc = jnp.where(jnp.logical_and(ki_c <= qi_c, ki_c < nq), sc, -jnp.inf)
    _, l, acc = softmax_update(mla, sc, cn)
    full = _dot((acc / l).astype(BF16), wuv_ref[...])
    head = lax.broadcasted_iota(jnp.int32, (c_rows, MLA_V), 0) // nq
    out = jnp.zeros((c_rows, MLA_V), F32)
    for hh in range(MLA_HEADS):
        out = jnp.where(head == hh, full[:, hh * MLA_V:(hh + 1) * MLA_V], out)
    oc_ref[...] = out

    gate = _dot(qx, means, precision=HIGHEST)
    sel = _topk_mask(gate, a_lane < nb, a_lane.astype(F32), min(MOBA_TOPK, nb)) > 0.5
    qi_a = lax.broadcasted_iota(jnp.int32, (a_rows, PAGE_SIZE), 0) % nq
    ki_a = lax.broadcasted_iota(jnp.int32, (a_rows, PAGE_SIZE), 1)
    s_own = _dot(qxb, mknt_ref[...].astype(BF16)) - slope * (qi_a - ki_a).astype(F32)
    s_own = jnp.where(jnp.logical_and(qi_a >= ki_a, ki_a < nq), s_own, -jnp.inf)
    m_own = jnp.max(s_own, axis=1, keepdims=True)
    p_own = jnp.exp(s_own - m_own)
    l_own = jnp.sum(p_own, axis=1, keepdims=True)
    o_own = _dot_t(p_own.astype(BF16), mvnt_ref[...].astype(BF16))
    m_all = jnp.maximum(jnp.max(jnp.where(sel, mblk, -jnp.inf), axis=1, keepdims=True), m_own)
    w = jnp.where(sel, jnp.exp(mblk - m_all), 0.0)
    w_own = jnp.exp(m_own - m_all)
    l_all = jnp.sum(w * lblk, axis=1, keepdims=True) + w_own * l_own
    o_all = w_own * o_own
    for jj in range(nb):
        o_all = o_all + w[:, jj:jj + 1] * oblk_ref[jj]
    o_all = o_all / l_all
    first_kv = lax.broadcasted_iota(jnp.int32, (a_rows, HEAD_DIM), 0) < MOBA_GROUP * nq
    oa_ref[...] = jnp.where(first_kv, o_all[:, :HEAD_DIM], o_all[:, HEAD_DIM:])


def _sample_attn(moba_in, sb_in, mla_in, caches_t, page_table, layer, nq):
    qx, mknt, mvnt = moba_in
    sq, sknt, svnt = sb_in
    ql, qp, cn, krnt, wuv = mla_in
    n, n_pages = page_table.shape
    past = n_pages * PAGE_SIZE
    assert past % MOBA_BLOCK == 0 and PAGES_PER_STEP % 2 == 0 and n_pages % PAGES_PER_STEP == 0
    nb = past // MOBA_BLOCK
    assert nb <= LANES
    a_rows, b_rows, c_rows = MOBA_HEADS * nq, SB_HEADS * nq, MLA_HEADS * nq
    width = MOBA_KV_HEADS * HEAD_DIM
    i = jnp.arange(1, MOBA_HEADS + 1, dtype=F32)
    slope_rows = jnp.broadcast_to(jnp.repeat(jnp.exp2(-8.0 * i / MOBA_HEADS), nq)[:, None], (a_rows, LANES))

    def seq(shape):
        nd = len(shape)
        return pl.BlockSpec((None,) + shape, lambda s_, pt: (s_,) + (0,) * nd)

    def const(shape):
        nd = len(shape)
        return pl.BlockSpec(shape, lambda s_, pt: (0,) * nd)

    hbm = pl.BlockSpec(memory_space=pl.ANY)
    pps = PAGES_PER_STEP
    grid_spec = pltpu.PrefetchScalarGridSpec(
        num_scalar_prefetch=1, grid=(n,),
        in_specs=[seq((a_rows, width)), seq((width, PAGE_SIZE)), seq((width, PAGE_SIZE)), const((a_rows, LANES)),
                  seq((b_rows, HEAD_DIM)), seq((HEAD_DIM, PAGE_SIZE)), seq((HEAD_DIM, PAGE_SIZE)),
                  const((KEY_TILE, KEY_TILE)),
                  seq((c_rows, MLA_KV_LORA)), seq((c_rows, MLA_ROPE)), seq((PAGE_SIZE, MLA_KV_LORA)),
                  seq((MLA_ROPE, PAGE_SIZE)), const(wuv.shape)] + [hbm] * 6,
        out_specs=[seq((a_rows, HEAD_DIM)), seq((b_rows, HEAD_DIM)), seq((c_rows, MLA_V))],
        scratch_shapes=[pltpu.VMEM((2, pps, width, PAGE_SIZE), F32), pltpu.VMEM((2, pps, width, PAGE_SIZE), F32),
                        pltpu.VMEM((2, pps, HEAD_DIM, PAGE_SIZE), F32), pltpu.VMEM((2, pps, HEAD_DIM, PAGE_SIZE), F32),
                        pltpu.VMEM((2, pps, PAGE_SIZE, MLA_KV_LORA), F32), pltpu.VMEM((2, pps, MLA_ROPE, PAGE_SIZE), F32),
                        pltpu.SemaphoreType.DMA((6, 2)),
                        pltpu.VMEM((nb, a_rows, width), F32)])
    return pl.pallas_call(
        functools.partial(_sample_attn_kernel, layer=layer, nq=nq, n_pages=n_pages, nb=nb, past=past),
        grid_spec=grid_spec,
        out_shape=[jax.ShapeDtypeStruct((n, a_rows, HEAD_DIM), F32), jax.ShapeDtypeStruct((n, b_rows, HEAD_DIM), F32),
                   jax.ShapeDtypeStruct((n, c_rows, MLA_V), F32)],
        compiler_params=_params("arbitrary"), name="sample_attn",
    )(page_table, qx, mknt, mvnt, slope_rows, sq, sknt, svnt, _neg_tri(KEY_TILE), ql, qp, cn, krnt, wuv, *caches_t)


def _layer_weights(l, p):
    w_in = p['w_in'][l]
    wa = w_in[:, :HI_COLS]
    wa_hi = wa.astype(BF16)
    half = MLA_ROPE // 2
    kpe0 = w_in.shape[1] - MLA_ROPE
    kpe_sw = jnp.concatenate([w_in[:, kpe0 + half:], w_in[:, kpe0:kpe0 + half]], axis=1)
    wq = p['mla_w_uq'][l].reshape(MLA_Q_LORA, MLA_HEADS, MLA_QK).transpose(1, 0, 2)
    wq_sw = jnp.concatenate([wq[..., :MLA_NOPE], wq[..., MLA_NOPE + half:], wq[..., MLA_NOPE:MLA_NOPE + half]], axis=-1)
    w_uk = p['mla_w_uk'][l].transpose(1, 0, 2)
    w_uv = p['mla_w_uv'][l]
    row = lambda a: a.reshape(1, -1)
    return dict(
        g1=row(p['norm1_g'][l]), wa_hi=wa_hi, wa_lo=(wa - wa_hi.astype(F32)).astype(BF16),
        wb=jnp.concatenate([w_in[:, HI_COLS:], kpe_sw], axis=1).astype(BF16),
        qn_g=row(p['mla_q_norm_g'][l]), wq6=wq.astype(BF16), wq6s=wq_sw.astype(BF16), kvn_g=row(p['mla_kv_norm_g'][l]),
        wk6=jnp.pad(w_uk, ((0, 0), (0, 0), (0, MLA_ROPE))).astype(BF16),
        ekr=jnp.concatenate([jnp.zeros((MLA_ROPE, MLA_NOPE), F32), jnp.eye(MLA_ROPE, dtype=F32)], axis=1).astype(BF16),
        wv6=w_uv.transpose(1, 0, 2).astype(BF16),
        wukt6=w_uk.transpose(0, 2, 1).astype(BF16),
        wuv_flat=w_uv.reshape(MLA_KV_LORA, MLA_HEADS * MLA_V).astype(BF16),
        g_out=row(p['out_norm_g'][l]), w_out=p['w_out'][l].astype(BF16), g2=row(p['norm2_g'][l]),
        wg=p['ffn_w_gate'][l].astype(BF16), wu=p['ffn_w_up'][l].astype(BF16), cw=p['ffn_conv_w'][l],
        cb=row(p['ffn_conv_b'][l]), wd=p['ffn_w_down'][l].astype(BF16), g_final=row(p['final_norm_g']))


def _rope_tables(pos):
    half = MLA_ROPE // 2
    inv = ROPE_THETA ** (-jnp.arange(half, dtype=F32) / half)
    ang = pos.astype(F32)[:, None] * inv[None, :]
    cos, sin = jnp.cos(ang), jnp.sin(ang)
    n = pos.shape[0]
    rc = jnp.concatenate([jnp.ones((n, MLA_NOPE), F32), cos, cos], axis=1)
    rs = jnp.concatenate([jnp.zeros((n, MLA_NOPE), F32), -sin, sin], axis=1)
    return rc, rs


def _prompt_forward(x, weights, depth):
    b, s, d = x.shape
    assert s % MOBA_BLOCK == 0
    t = b * s
    tm = min(256, s)
    rc, rs = _rope_tables(jnp.arange(s))
    rc, rs = jnp.tile(rc, (b, 1)), jnp.tile(rs, (b, 1))
    x2d = x.reshape(t, d)
    st = [[] for _ in range(7)]
    for l in range(depth):
        w = weights[l]
        (qa6, ka, ka2, va, va2, qb4, kb, vb, ckv, kpe, mq6, mk6, mv6) = _project(x2d, rc, rs, w, False, tm)
        oa6 = _moba_prompt(qa6, ka2, va2, b, s)
        ob4 = _sb_prompt(qb4, kb, vb, b, s)
        oc6 = _mla_prompt(mq6, mk6, mv6, b, s)
        x2d, gl = _merge_ffn(x2d, oa6, ob4, oc6, w, l == depth - 1, tm, tiles_per_seq=s // tm)
        buf = gl.reshape(b, s // tm, 8, -1)[:, -1, 8 - (CONV_W - 1):, :]
        for lst, a in zip(st, (ka.reshape(b, s, MOBA_KV_HEADS, HEAD_DIM), va.reshape(b, s, MOBA_KV_HEADS, HEAD_DIM),
                               kb.reshape(b, s, HEAD_DIM), vb.reshape(b, s, HEAD_DIM),
                               ckv.reshape(b, s, MLA_KV_LORA), kpe.reshape(b, s, MLA_ROPE), buf)):
            lst.append(a)
    return x2d.reshape(b, s, d), [jnp.stack(a, axis=0) for a in st]


def _pad_tokens_t(a, n, nq):
    a = a.reshape(n, nq, -1).transpose(0, 2, 1)
    return jnp.pad(a, ((0, 0), (0, 0), (0, PAGE_SIZE - nq)))


def _rows_per_seq(a, n, nq):
    h = a.shape[0]
    return a.reshape(h, n, nq, -1).transpose(1, 0, 2, 3).reshape(n, h * nq, -1)


def _heads_major(a, n, nq, h):
    return a.reshape(n, h, nq, -1).transpose(1, 0, 2, 3).reshape(h, n * nq, -1)


def _sample_forward(x, caches, s_conv, page_table, weights, depth):
    n, nq, d = x.shape
    assert nq <= 8 and CONV_W - 1 <= nq
    c_mk, c_mv, c_sk, c_sv, c_mc, c_mr = caches
    n_pages = page_table.shape[1]
    past = n_pages * PAGE_SIZE
    t = n * nq
    tm = min(256, t)
    rc, rs = _rope_tables(past + jnp.arange(nq))
    rc, rs = jnp.tile(rc, (n, 1)), jnp.tile(rs, (n, 1))
    width = MOBA_KV_HEADS * HEAD_DIM
    mk_t = c_mk.transpose(0, 1, 3, 4, 2).reshape(c_mk.shape[0], c_mk.shape[1], width, PAGE_SIZE)
    mv_t = c_mv.transpose(0, 1, 3, 4, 2).reshape(c_mv.shape[0], c_mv.shape[1], width, PAGE_SIZE)
    sk_t = c_sk.transpose(0, 1, 3, 2)
    sv_t = c_sv.transpose(0, 1, 3, 2)
    mr_t = c_mr.transpose(0, 1, 3, 2)
    x2d = x.reshape(t, d)
    st = [[] for _ in range(7)]
    for l in range(depth):
        w = weights[l]
        (qa6, ka, va, qb4, kb, vb, ckv, kpe, qlat6, qpe6) = _project(x2d, rc, rs, w, True, tm)
        qa = _rows_per_seq(qa6, n, nq)
        kvh = (jnp.arange(MOBA_HEADS * nq) // (MOBA_GROUP * nq))[None, :, None]
        qx = jnp.concatenate([jnp.where(kvh == 0, qa, 0.0), jnp.where(kvh == 1, qa, 0.0)], axis=-1)
        cn = jnp.pad(ckv.reshape(n, nq, MLA_KV_LORA), ((0, 0), (0, PAGE_SIZE - nq), (0, 0)))
        oa, ob, oc = _sample_attn(
            (qx, _pad_tokens_t(ka, n, nq), _pad_tokens_t(va, n, nq)),
            (_rows_per_seq(qb4, n, nq), _pad_tokens_t(kb, n, nq), _pad_tokens_t(vb, n, nq)),
            (_rows_per_seq(qlat6, n, nq), _rows_per_seq(qpe6, n, nq), cn, _pad_tokens_t(kpe, n, nq), w['wuv_flat']),
            (mk_t, mv_t, sk_t, sv_t, c_mc, mr_t), page_table, l, nq)
        buf = s_conv[l]
        zero = jnp.zeros((n, nq - 1, buf.shape[-1]), F32)
        st1 = jnp.concatenate([buf[:, 1:2], zero], axis=1).reshape(t, -1)
        st2 = jnp.concatenate([buf, zero[:, :nq - 2]], axis=1).reshape(t, -1)
        x2d, g = _merge_ffn(x2d, _heads_major(oa, n, nq, MOBA_HEADS), _heads_major(ob, n, nq, SB_HEADS),
                            _heads_major(oc, n, nq, MLA_HEADS), w, l == depth - 1, tm, seq_rows=nq, st=(st1, st2))
        new_buf = g.reshape(n, nq, -1)[:, nq - (CONV_W - 1):, :]
        for lst, a in zip(st, (ka.reshape(n, nq, MOBA_KV_HEADS, HEAD_DIM), va.reshape(n, nq, MOBA_KV_HEADS, HEAD_DIM),
                               kb.reshape(n, nq, HEAD_DIM), vb.reshape(n, nq, HEAD_DIM),
                               ckv.reshape(n, nq, MLA_KV_LORA), kpe.reshape(n, nq, MLA_ROPE), new_buf)):
            lst.append(a)
    return x2d.reshape(n, nq, d), [jnp.stack(a, axis=0) for a in st]


def kernel(x_prompt, x_sample, cache_moba_k, cache_moba_v, cache_sb_k, cache_sb_v, cache_mla_latent, cache_mla_krope, state_ffn_conv, page_table, norm1_g, w_in, mla_q_norm_g, mla_w_uq, mla_kv_norm_g, mla_w_uk, mla_w_uv, out_norm_g, w_out, norm2_g, ffn_w_gate, ffn_w_up, ffn_conv_w, ffn_conv_b, ffn_w_down, final_norm_g):
    p = dict(norm1_g=norm1_g, w_in=w_in, mla_q_norm_g=mla_q_norm_g, mla_w_uq=mla_w_uq,
             mla_kv_norm_g=mla_kv_norm_g, mla_w_uk=mla_w_uk, mla_w_uv=mla_w_uv, out_norm_g=out_norm_g,
             w_out=w_out, norm2_g=norm2_g, ffn_w_gate=ffn_w_gate, ffn_w_up=ffn_w_up,
             ffn_conv_w=ffn_conv_w, ffn_conv_b=ffn_conv_b, ffn_w_down=ffn_w_down, final_norm_g=final_norm_g)
    depth = w_in.shape[0]
    weights = [_layer_weights(l, p) for l in range(depth)]
    y_prompt, pst = _prompt_forward(x_prompt, weights, depth)
    y_sample, sst = _sample_forward(
        x_sample, (cache_moba_k, cache_moba_v, cache_sb_k, cache_sb_v, cache_mla_latent, cache_mla_krope),
        state_ffn_conv, page_table, weights, depth)
    return (y_prompt, y_sample, *pst, *sst)
```

```python
import functools

import jax
import jax.numpy as jnp
from jax import lax
from jax.experimental import pallas as pl
from jax.experimental.pallas import tpu as pltpu

F32 = jnp.float32
BF16 = jnp.bfloat16
HIGHEST = lax.Precision.HIGHEST

HEAD_DIM = 64
MOBA_HEADS = 6
MOBA_KV_HEADS = 2
MOBA_GROUP = MOBA_HEADS // MOBA_KV_HEADS
MOBA_BLOCK = 256
MOBA_TOPK = 3
SB_HEADS = 4
MLA_HEADS = 6
MLA_Q_LORA = 256
MLA_KV_LORA = 256
MLA_NOPE = 64
MLA_ROPE = 32
MLA_QK = MLA_NOPE + MLA_ROPE
MLA_V = 64
ROPE_THETA = 10000.0
MIX_A = MOBA_HEADS * HEAD_DIM
MIX_B = SB_HEADS * HEAD_DIM
MIX_C = MLA_HEADS * MLA_V
CONV_W = 3
RMS_EPS = 1e-6
PAGE_SIZE = 128

V7X_VMEM_BYTES = 64 * 1024 * 1024
VMEM_LIMIT = (V7X_VMEM_BYTES * 7) // 8
LANES = 128

KEY_TILE = 256
MASK_NEG = -1e30
PAGES_PER_STEP = 16
HI_COLS = MIX_A + MOBA_KV_HEADS * HEAD_DIM


def _dot(a, b, precision=None):
    return jnp.dot(a, b, preferred_element_type=F32, precision=precision)


def _dot_t(a, b, precision=None):
    return lax.dot_general(a, b, (((1,), (1,)), ((), ())), preferred_element_type=F32, precision=precision)


def _rms(x, g):
    return x * lax.rsqrt(jnp.mean(x * x, axis=-1, keepdims=True) + RMS_EPS) * g


def _params(*sem):
    return pltpu.CompilerParams(dimension_semantics=sem, vmem_limit_bytes=VMEM_LIMIT)


def _const_spec(shape):
    nd = len(shape)
    return pl.BlockSpec(shape, lambda *_: (0,) * nd)


def _proj_kernel(*refs, absorbed):
    (x_ref, g1_ref, wah_ref, wal_ref, wb_ref, qng_ref, wq_ref, wqs_ref, kvg_ref, rc_ref, rs_ref) = refs[:11]
    if absorbed:
        (wukt_ref, qa_ref, ka_ref, va_ref, qb_ref, kb_ref, vb_ref, ckv_ref, kpe_ref, qlat_ref, qpe_ref) = refs[11:]
    else:
        (wk_ref, ekr_ref, wv_ref, qa_ref, ka_ref, ka2_ref, va_ref, va2_ref, qb_ref, kb_ref, vb_ref, ckv_ref, kpe_ref,
         mq_ref, mk_ref, mv_ref) = refs[11:]
    h = _rms(x_ref[...], g1_ref[...])
    h_hi = h.astype(BF16)
    h_lo = (h - h_hi.astype(F32)).astype(BF16)
    pa = _dot(h_hi, wah_ref[...]) + (_dot(h_lo, wah_ref[...]) + _dot(h_hi, wal_ref[...]))
    pb = _dot(h_hi, wb_ref[...])
    for k in range(MOBA_HEADS):
        qa_ref[k] = pa[:, k * HEAD_DIM:(k + 1) * HEAD_DIM] * (HEAD_DIM ** -0.5)
    ka = pa[:, MIX_A:MIX_A + 2 * HEAD_DIM]
    ka_ref[...] = ka
    va = pb[:, 0:128]
    va_ref[...] = va
    if not absorbed:
        for k in range(MOBA_KV_HEADS):
            ka2_ref[k] = ka[:, k * HEAD_DIM:(k + 1) * HEAD_DIM]
        va2_ref[0] = va.T.astype(BF16)
    for k in range(SB_HEADS):
        qb_ref[k] = (pb[:, 128 + k * HEAD_DIM:128 + (k + 1) * HEAD_DIM] * (HEAD_DIM ** -0.5)).astype(BF16)
    kb_ref[...] = pb[:, 384:448]
    vb_ref[...] = pb[:, 448:512]
    cqn = _rms(pb[:, 512:768], qng_ref[...]).astype(BF16)
    ckvn = _rms(pb[:, 768:1024], kvg_ref[...])
    ckv_ref[...] = ckvn
    rc = rc_ref[...]
    rs = rs_ref[...]
    kr = pb[:, 1024:1056] * rc[:, MLA_NOPE:] + pb[:, 1056:1088] * rs[:, MLA_NOPE:]
    kpe_ref[...] = kr
    scale = MLA_QK ** -0.5
    if not absorbed:
        ckvn_b = ckvn.astype(BF16)
        kr_b = kr.astype(BF16)
    for hh in range(MLA_HEADS):
        qh = (_dot(cqn, wq_ref[hh]) * rc + _dot(cqn, wqs_ref[hh]) * rs) * scale
        if absorbed:
            qlat_ref[hh] = _dot(qh[:, :MLA_NOPE].astype(BF16), wukt_ref[hh])
            qpe_ref[hh] = qh[:, MLA_NOPE:]
        else:
            mq_ref[hh] = qh.astype(BF16)
            mk_ref[hh] = (_dot(ckvn_b, wk_ref[hh]) + _dot(kr_b, ekr_ref[...])).astype(BF16)
            mv_ref[hh, 0] = _dot_t(wv_ref[hh], ckvn_b).astype(BF16)


def _project(x2d, rc, rs, w, absorbed, tm):
    t, d = x2d.shape
    grid = (t // tm,)
    rows = lambda width: pl.BlockSpec((tm, width), lambda i: (i, 0))
    heads = lambda nh, width: pl.BlockSpec((nh, tm, width), lambda i: (0, i, 0))
    ins = [x2d, w['g1'], w['wa_hi'], w['wa_lo'], w['wb'], w['qn_g'], w['wq6'], w['wq6s'], w['kvn_g'], rc, rs]
    in_specs = [rows(d)] + [_const_spec(a.shape) for a in ins[1:9]] + [rows(MLA_QK), rows(MLA_QK)]
    if absorbed:
        extra = [w['wukt6']]
        outs = [((MOBA_HEADS, t, HEAD_DIM), F32, heads(MOBA_HEADS, HEAD_DIM)),
                ((t, 128), F32, rows(128)), ((t, 128), F32, rows(128)),
                ((SB_HEADS, t, HEAD_DIM), BF16, heads(SB_HEADS, HEAD_DIM)),
                ((t, HEAD_DIM), F32, rows(HEAD_DIM)), ((t, HEAD_DIM), F32, rows(HEAD_DIM)),
                ((t, MLA_KV_LORA), F32, rows(MLA_KV_LORA)), ((t, MLA_ROPE), F32, rows(MLA_ROPE)),
                ((MLA_HEADS, t, MLA_KV_LORA), F32, heads(MLA_HEADS, MLA_KV_LORA)),
                ((MLA_HEADS, t, MLA_ROPE), F32, heads(MLA_HEADS, MLA_ROPE))]
    else:
        extra = [w['wk6'], w['ekr'], w['wv6t']]
        outs = [((MOBA_HEADS, t, HEAD_DIM), F32, heads(MOBA_HEADS, HEAD_DIM)),
                ((t, 128), F32, rows(128)), ((MOBA_KV_HEADS, t, HEAD_DIM), F32, heads(MOBA_KV_HEADS, HEAD_DIM)),
                ((t, 128), F32, rows(128)),
                ((t // tm, 128, tm), BF16, pl.BlockSpec((1, 128, tm), lambda i: (i, 0, 0))),
                ((SB_HEADS, t, HEAD_DIM), BF16, heads(SB_HEADS, HEAD_DIM)),
                ((t, HEAD_DIM), F32, rows(HEAD_DIM)), ((t, HEAD_DIM), F32, rows(HEAD_DIM)),
                ((t, MLA_KV_LORA), F32, rows(MLA_KV_LORA)), ((t, MLA_ROPE), F32, rows(MLA_ROPE)),
                ((MLA_HEADS, t, MLA_QK), BF16, heads(MLA_HEADS, MLA_QK)),
                ((MLA_HEADS, t, MLA_QK), BF16, heads(MLA_HEADS, MLA_QK)),
                ((MLA_HEADS, t // tm, MLA_V, tm), BF16,
                 pl.BlockSpec((MLA_HEADS, 1, MLA_V, tm), lambda i: (0, i, 0, 0)))]
    ins += extra
    in_specs += [_const_spec(a.shape) for a in extra]
    return pl.pallas_call(
        functools.partial(_proj_kernel, absorbed=absorbed),
        grid=grid, in_specs=in_specs,
        out_specs=[o[2] for o in outs],
        out_shape=[jax.ShapeDtypeStruct(o[0], o[1]) for o in outs],
        compiler_params=_params("parallel"),
        name="proj_sample" if absorbed else "proj_prompt",
    )(*ins)


def _topk_mask(gate, allowed, col_f, k_sel):
    g = jnp.where(allowed, gate, -jnp.inf)
    sel = jnp.zeros(gate.shape, F32)
    for _ in range(k_sel):
        m = jnp.max(g, axis=1, keepdims=True)
        cand = jnp.where(g == m, col_f, 1e9)
        cand = jnp.where(m > -jnp.inf, cand, 1e9)
        first = jnp.min(cand, axis=1, keepdims=True)
        pick = col_f == first
        sel = jnp.where(pick, 1.0, sel)
        g = jnp.where(pick, -jnp.inf, g)
    return sel


def _moba_prompt_kernel(q_ref, k_ref, vt_ref, slope_ref, d0_ref, o_ref, means_ref, kext_ref, *, tq, nb, ext):
    c = pl.program_id(1)
    rows = MOBA_GROUP * tq
    kv_heads = range(MOBA_KV_HEADS)

    @pl.when(c == 0)
    def _build():
        means_ref[...] = jnp.zeros(means_ref.shape, F32)
        col = lax.broadcasted_iota(jnp.int32, (MOBA_BLOCK, ext), 1)

        def body(j, carry):
            r0 = pl.multiple_of(j * MOBA_BLOCK, MOBA_BLOCK)
            onehot = jnp.where(col == j, 1.0, 0.0)
            for hk in kv_heads:
                kb = k_ref[hk, pl.ds(r0, MOBA_BLOCK), :]
                means_ref[hk, pl.ds(j, 1), :] = jnp.sum(kb, axis=0, keepdims=True) * (1.0 / MOBA_BLOCK)
                kext_ref[hk, pl.ds(r0, MOBA_BLOCK), :] = jnp.concatenate([kb, onehot], axis=1).astype(BF16)
            return carry

        lax.fori_loop(0, nb, body, 0)

    n_past = (c * tq) // MOBA_BLOCK
    col_i = lax.broadcasted_iota(jnp.int32, (rows, ext), 1)
    q_ext = []
    for hk in kv_heads:
        q = q_ref[hk * MOBA_GROUP:(hk + 1) * MOBA_GROUP].reshape(rows, HEAD_DIM)
        gate = _dot_t(q, means_ref[hk], precision=HIGHEST)
        sel = _topk_mask(gate, col_i < n_past, col_i.astype(F32), min(MOBA_TOPK, nb))
        keep = jnp.logical_or(sel > 0.5, col_i == n_past)
        q_ext.append(jnp.concatenate([q, jnp.where(keep, 0.0, MASK_NEG)], axis=1).astype(BF16))
    d0 = d0_ref[...]

    def block(j, carry, causal):
        r0 = pl.multiple_of(j * MOBA_BLOCK, MOBA_BLOCK)
        d = d0 + (c * tq - j * MOBA_BLOCK).astype(F32)
        ss = [_dot_t(kext_ref[hk, pl.ds(r0, MOBA_BLOCK), :], q_ext[hk]) for hk in kv_heads]
        mid = []
        for hk in kv_heads:
            m, l, acc = carry[hk]
            s = ss[hk] - slope_ref[hk] * d
            if causal:
                s = jnp.where(d >= 0, s, -jnp.inf)
            m_new = jnp.maximum(m, jnp.max(s, axis=0, keepdims=True))
            p = jnp.exp(s - m_new)
            alpha = jnp.exp(m - m_new)
            mid.append((m_new, alpha * l + jnp.sum(p, axis=0, keepdims=True), alpha * acc, p.astype(BF16)))
        return tuple((m_new, l, acc + _dot(vt_ref[j, hk * HEAD_DIM:(hk + 1) * HEAD_DIM, :], p))
                     for hk, (m_new, l, acc, p) in enumerate(mid))

    init = tuple((jnp.full((1, rows), -jnp.inf, F32), jnp.zeros((1, rows), F32), jnp.zeros((HEAD_DIM, rows), F32))
                 for _ in kv_heads)
    carry = lax.fori_loop(0, n_past, lambda j, cr: block(j, cr, False), init)
    carry = block(n_past, carry, True)
    heads = []
    for hk in kv_heads:
        _, l, acc = carry[hk]
        o = acc / l
        heads += [o[:, g * tq:(g + 1) * tq] for g in range(MOBA_GROUP)]
    for pair in range(MOBA_HEADS // 2):
        o_ref[pair] = jnp.concatenate(heads[2 * pair:2 * pair + 2], axis=0).T


def _moba_prompt(qa6, ka2, vat, b, s, tq=128):
    nq = s // tq
    nb = s // MOBA_BLOCK
    ext = max(32, -(-nb // 32) * 32)
    rows = MOBA_GROUP * tq
    i = jnp.arange(1, MOBA_HEADS + 1, dtype=F32)
    slopes = jnp.exp2(-8.0 * i / MOBA_HEADS).reshape(MOBA_KV_HEADS, MOBA_GROUP)
    slope_full = jnp.broadcast_to(jnp.repeat(slopes, tq, axis=1)[:, None, :], (MOBA_KV_HEADS, MOBA_BLOCK, rows))
    qi = jnp.tile(jnp.arange(tq, dtype=F32), MOBA_GROUP)
    d0 = qi[None, :] - jnp.arange(MOBA_BLOCK, dtype=F32)[:, None]
    once = pl.Buffered(1)
    return pl.pallas_call(
        functools.partial(_moba_prompt_kernel, tq=tq, nb=nb, ext=ext),
        grid=(b, nq),
        in_specs=[pl.BlockSpec((MOBA_HEADS, tq, HEAD_DIM), lambda bi, ci: (0, bi * nq + ci, 0)),
                  pl.BlockSpec((MOBA_KV_HEADS, s, HEAD_DIM), lambda bi, ci: (0, bi, 0), pipeline_mode=once),
                  pl.BlockSpec((nb, MOBA_KV_HEADS * HEAD_DIM, MOBA_BLOCK), lambda bi, ci: (bi, 0, 0),
                               pipeline_mode=once),
                  _const_spec((MOBA_KV_HEADS, MOBA_BLOCK, rows)),
                  _const_spec((MOBA_BLOCK, rows))],
        out_specs=pl.BlockSpec((MOBA_HEADS // 2, tq, 2 * HEAD_DIM), lambda bi, ci: (0, bi * nq + ci, 0)),
        out_shape=jax.ShapeDtypeStruct((MOBA_HEADS // 2, b * s, 2 * HEAD_DIM), F32),
        scratch_shapes=[pltpu.VMEM((MOBA_KV_HEADS, ext, HEAD_DIM), F32),
                        pltpu.VMEM((MOBA_KV_HEADS, s, HEAD_DIM + ext), BF16)],
        compiler_params=_params("parallel", "arbitrary"),
        name="moba_prompt",
    )(qa6, ka2, vat, slope_full, d0)


def _sb_tile(z, mask, neg_tri, r_run):
    sp = jnp.maximum(z, 0.0) + jnp.log(1.0 + jnp.exp(-jnp.abs(z)))
    if mask is not None:
        sp = jnp.where(mask, sp, 0.0)
    tail = _dot(sp.astype(BF16), neg_tri)
    a = jnp.exp((z - sp) + tail + r_run)
    if mask is not None:
        a = jnp.where(mask, a, 0.0)
    return a, r_run - jnp.sum(sp, axis=1, keepdims=True)


def _sb_prompt_kernel(q_ref, k_ref, v_ref, tri_ref, o_ref, kb_ref, vb_ref, *, tq, n_chains):
    c = pl.program_id(1)

    @pl.when(c == 0)
    def _cast():
        kb_ref[...] = k_ref[...].astype(BF16)
        vb_ref[...] = v_ref[...].astype(BF16)

    hpc = SB_HEADS // n_chains
    rows = hpc * tq
    qs = [q_ref[ch * hpc:(ch + 1) * hpc].reshape(rows, HEAD_DIM) for ch in range(n_chains)]
    tri = tri_ref[...]
    jd = (c * tq) // KEY_TILE

    def tile(j, carry, masked):
        r0 = pl.multiple_of(j * KEY_TILE, KEY_TILE)
        kt = kb_ref[pl.ds(r0, KEY_TILE), :]
        vt = vb_ref[pl.ds(r0, KEY_TILE), :]
        mask = None
        if masked:
            qpos = c * tq + (lax.broadcasted_iota(jnp.int32, (rows, KEY_TILE), 0) & (tq - 1))
            kpos = j * KEY_TILE + lax.broadcasted_iota(jnp.int32, (rows, KEY_TILE), 1)
            mask = kpos < qpos
        zs = [_dot_t(qs[ch], kt) for ch in range(n_chains)]
        sps = []
        for z in zs:
            sp = jnp.maximum(z, 0.0) + jnp.log(1.0 + jnp.exp(-jnp.abs(z)))
            sps.append(jnp.where(mask, sp, 0.0) if masked else sp)
        tails = [_dot(sp.astype(BF16), tri) for sp in sps]
        out = []
        for ch in range(n_chains):
            r_run, acc = carry[ch]
            a = jnp.exp((zs[ch] - sps[ch]) + tails[ch] + r_run)
            if masked:
                a = jnp.where(mask, a, 0.0)
            out.append((r_run - jnp.sum(sps[ch], axis=1, keepdims=True), a.astype(BF16), acc))
        return tuple((r_run, acc + _dot(a, vt)) for r_run, a, acc in out)

    init = tuple((jnp.zeros((rows, 1), F32), jnp.zeros((rows, HEAD_DIM), F32)) for _ in range(n_chains))
    carry = tile(jd, init, True)
    carry = lax.fori_loop(0, jd, lambda i, cr: tile(jd - 1 - i, cr, False), carry)
    for ch in range(n_chains):
        o_ref[ch * hpc:(ch + 1) * hpc] = carry[ch][1].reshape(hpc, tq, HEAD_DIM)


def _neg_tri(n):
    r = jnp.arange(n)
    return -((r[:, None] > r[None, :]).astype(BF16))


def _sb_prompt(qb4, kb, vb, b, s, tq=128, n_chains=2):
    nq = s // tq
    return pl.pallas_call(
        functools.partial(_sb_prompt_kernel, tq=tq, n_chains=n_chains),
        grid=(b, nq),
        in_specs=[pl.BlockSpec((SB_HEADS, tq, HEAD_DIM), lambda bi, ci: (0, bi * nq + ci, 0)),
                  pl.BlockSpec((s, HEAD_DIM), lambda bi, ci: (bi, 0)),
                  pl.BlockSpec((s, HEAD_DIM), lambda bi, ci: (bi, 0)),
                  _const_spec((KEY_TILE, KEY_TILE))],
        out_specs=pl.BlockSpec((SB_HEADS, tq, HEAD_DIM), lambda bi, ci: (0, bi * nq + ci, 0)),
        out_shape=jax.ShapeDtypeStruct((SB_HEADS, b * s, HEAD_DIM), F32),
        scratch_shapes=[pltpu.VMEM((s, HEAD_DIM), BF16), pltpu.VMEM((s, HEAD_DIM), BF16)],
        compiler_params=_params("parallel", "arbitrary"),
        name="sb_prompt",
    )(qb4, kb, vb, _neg_tri(KEY_TILE))


def _mla_prompt_kernel(q_ref, k_ref, vt_ref, o_ref, *, tq):
    c = pl.program_id(1)
    nh = MLA_HEADS
    qs = [q_ref[h] for h in range(nh)]
    jd = (c * tq) // KEY_TILE

    def tile(j, carry, masked):
        r0 = pl.multiple_of(j * KEY_TILE, KEY_TILE)
        if masked:
            kpos = j * KEY_TILE + lax.broadcasted_iota(jnp.int32, (KEY_TILE, tq), 0)
            qpos = c * tq + lax.broadcasted_iota(jnp.int32, (KEY_TILE, tq), 1)
            visible = kpos <= qpos
        ss = [_dot_t(k_ref[h, pl.ds(r0, KEY_TILE), :], qs[h]) for h in range(nh)]
        mid = []
        for h in range(nh):
            m, l, acc = carry[h]
            s = jnp.where(visible, ss[h], -jnp.inf) if masked else ss[h]
            m_new = jnp.maximum(m, jnp.max(s, axis=0, keepdims=True))
            p = jnp.exp(s - m_new)
            alpha = jnp.exp(m - m_new)
            mid.append((m_new, alpha * l + jnp.sum(p, axis=0, keepdims=True), alpha * acc, p.astype(BF16)))
        return tuple((m_new, l, acc + _dot(vt_ref[h, j], p)) for h, (m_new, l, acc, p) in enumerate(mid))

    init = tuple((jnp.full((1, tq), -jnp.inf, F32), jnp.zeros((1, tq), F32), jnp.zeros((MLA_V, tq), F32))
                 for _ in range(nh))
    carry = tile(jd, init, True)
    carry = lax.fori_loop(0, jd, lambda j, cr: tile(j, cr, False), carry)
    for pair in range(nh // 2):
        outs = [carry[2 * pair + e][2] / carry[2 * pair + e][1] for e in range(2)]
        o_ref[pair] = jnp.concatenate(outs, axis=0).T


def _mla_prompt(mq6, mk6, mvt6, b, s):
    tq = KEY_TILE
    nq = s // tq
    once = pl.Buffered(1)
    return pl.pallas_call(
        functools.partial(_mla_prompt_kernel, tq=tq),
        grid=(b, nq),
        in_specs=[pl.BlockSpec((MLA_HEADS, tq, MLA_QK), lambda bi, ci: (0, bi * nq + ci, 0)),
                  pl.BlockSpec((MLA_HEADS, s, MLA_QK), lambda bi, ci: (0, bi, 0), pipeline_mode=once),
                  pl.BlockSpec((MLA_HEADS, nq, MLA_V, KEY_TILE), lambda bi, ci: (0, bi, 0, 0), pipeline_mode=once)],
        out_specs=pl.BlockSpec((MLA_HEADS // 2, tq, 2 * MLA_V), lambda bi, ci: (0, bi * nq + ci, 0)),
        out_shape=jax.ShapeDtypeStruct((MLA_HEADS // 2, b * s, 2 * MLA_V), F32),
        compiler_params=_params("parallel", "arbitrary"),
        name="mla_prompt",
    )(mq6, mk6, mvt6)


def _merge_ffn_kernel(*refs, tm, seq_rows, tiles_per_seq, n_chunks, final):
    (x_ref, oa_ref, ob_ref, oc_ref, gout_ref, wout_ref, g2_ref, wg_ref, wu_ref, cw_ref, cb_ref, wd_ref, gf_ref) = refs[:13]
    carry_mode = seq_rows is None
    if carry_mode:
        y_ref, gl_ref, gtail_ref = refs[13:]
    else:
        st1_ref, st2_ref, y_ref, gl_ref = refs[13:]
    i = pl.program_id(0)
    if carry_mode:
        @pl.when(i == 0)
        def _init():
            gtail_ref[...] = jnp.zeros(gtail_ref.shape, F32)

    def group(ref):
        parts = [ref[k] for k in range(ref.shape[0])]
        ssq = parts[0] * parts[0]
        for p in parts[1:]:
            ssq = ssq + p * p
        r = lax.rsqrt(jnp.sum(ssq, axis=-1, keepdims=True) * (1.0 / (ref.shape[0] * ref.shape[2])) + RMS_EPS)
        return [p * r for p in parts]

    o = jnp.concatenate(group(oa_ref) + group(ob_ref) + group(oc_ref), axis=-1)
    x1 = x_ref[...] + _dot((o * gout_ref[...]).astype(BF16), wout_ref[...])
    h2 = _rms(x1, g2_ref[...]).astype(BF16)
    d_ff = wg_ref.shape[1]
    fc = d_ff // n_chunks
    row = lax.broadcasted_iota(jnp.int32, (tm, 1), 0)
    y = x1
    for ci in range(n_chunks):
        sl = slice(ci * fc, (ci + 1) * fc)
        g = _dot(h2, wg_ref[:, sl])
        u = _dot(h2, wu_ref[:, sl])
        if carry_mode:
            prev = jnp.where(i % tiles_per_seq == 0, 0.0, gtail_ref[:, sl])
            p6, p7 = prev[6:7, :], prev[7:8, :]
            hist1 = jnp.where(row == 0, p7, pltpu.roll(g, 1, 0))
            hist2 = jnp.where(row == 0, p6, jnp.where(row == 1, p7, pltpu.roll(g, 2, 0)))
            gtail_ref[:, sl] = g[tm - 8:, :]
            gl_ref[:, sl] = g[tm - 8:, :]
        else:
            rpos = row % seq_rows
            hist1 = jnp.where(rpos == 0, st1_ref[:, sl], pltpu.roll(g, 1, 0))
            hist2 = jnp.where(rpos < 2, st2_ref[:, sl], pltpu.roll(g, 2, 0))
            gl_ref[:, sl] = g
        cw = cw_ref[:, sl]
        conv = cb_ref[:, sl] + hist2 * cw[0:1, :]
        conv = conv + hist1 * cw[1:2, :]
        conv = conv + g * cw[2:3, :]
        act = conv * (1.0 / (1.0 + jnp.exp(-conv))) * u
        y = y + _dot(act.astype(BF16), wd_ref[sl, :])
    if final:
        y = _rms(y, gf_ref[...])
    y_ref[...] = y


def _merge_ffn(x2d, oa6, ob4, oc6, w, final, tm, seq_rows=None, tiles_per_seq=1, st=None):
    t, d = x2d.shape
    d_ff = w['wg'].shape[1]
    n_chunks = 2
    nt = t // tm
    carry_mode = seq_rows is None
    rows = lambda width: pl.BlockSpec((tm, width), lambda i: (i, 0))
    slabs = lambda a: pl.BlockSpec((a.shape[0], tm, a.shape[2]), lambda i: (0, i, 0))
    ins = [x2d, oa6, ob4, oc6, w['g_out'], w['w_out'], w['g2'], w['wg'], w['wu'], w['cw'], w['cb'], w['wd'], w['g_final']]
    in_specs = [rows(d), slabs(oa6), slabs(ob4), slabs(oc6)]
    in_specs += [pl.BlockSpec(a.shape, lambda i, nd=a.ndim: (0,) * nd, pipeline_mode=pl.Buffered(1)) for a in ins[4:]]
    scratch = []
    if carry_mode:
        gl_shape, gl_spec = (nt * 8, d_ff), pl.BlockSpec((8, d_ff), lambda i: (i, 0))
        scratch = [pltpu.VMEM((8, d_ff), F32)]
    else:
        ins += list(st)
        in_specs += [rows(d_ff), rows(d_ff)]
        gl_shape, gl_spec = (t, d_ff), rows(d_ff)
    return pl.pallas_call(
        functools.partial(_merge_ffn_kernel, tm=tm, seq_rows=seq_rows, tiles_per_seq=tiles_per_seq,
                          n_chunks=n_chunks, final=final),
        grid=(nt,), in_specs=in_specs,
        out_specs=[rows(d), gl_spec],
        out_shape=[jax.ShapeDtypeStruct((t, d), F32), jax.ShapeDtypeStruct(gl_shape, F32)],
        scratch_shapes=scratch,
        compiler_params=_params("arbitrary"),
        name="merge_ffn_prompt" if carry_mode else "merge_ffn_sample",
    )(*ins)


def _sample_attn_kernel(pt_ref,
                        qx_ref, mknt_ref, mvnt_ref, slope_ref,
                        sq_ref, sknt_ref, svnt_ref, ntri_ref,
                        ql_ref, qp_ref, cn_ref, krnt_ref, wuv_ref,
                        mk_hbm, mv_hbm, sk_hbm, sv_hbm, lat_hbm, kr_hbm,
                        oa_ref, ob_ref, oc_ref,
                        mk_buf, mv_buf, sk_buf, sv_buf, lat_buf, kr_buf, sems,
                        oblk_ref,
                        *, layer, nq, n_pages, nb, past):
    n = pl.program_id(0)
    n_seq = pl.num_programs(0)
    pps = PAGES_PER_STEP
    n_steps = n_pages // pps
    hbms = (mk_hbm, mv_hbm, sk_hbm, sv_hbm, lat_hbm, kr_hbm)
    bufs = (mk_buf, mv_buf, sk_buf, sv_buf, lat_buf, kr_buf)

    def page_copies(seq, step, slot):
        out = []
        for i in range(pps):
            page = pt_ref[seq, n_pages - 1 - (step * pps + i)]
            for a in range(len(hbms)):
                out.append(pltpu.make_async_copy(hbms[a].at[layer, page], bufs[a].at[slot, i], sems.at[a, slot]))
        return out

    @pl.when(n == 0)
    def _first_fetch():
        for cp in page_copies(0, 0, 0):
            cp.start()

    a_rows = MOBA_HEADS * nq
    b_rows = SB_HEADS * nq
    c_rows = MLA_HEADS * nq
    width = MOBA_KV_HEADS * HEAD_DIM
    qx = qx_ref[...]
    qxb = qx.astype(BF16)
    slope = slope_ref[...]
    sq = sq_ref[...]
    ql = ql_ref[...].astype(BF16)
    qp = qp_ref[...].astype(BF16)
    a_lane = lax.broadcasted_iota(jnp.int32, (a_rows, LANES), 1)
    m_lane = lax.broadcasted_iota(jnp.int32, (width, LANES), 1)

    qi_b = lax.broadcasted_iota(jnp.int32, (b_rows, PAGE_SIZE), 0) % nq
    ki_b = lax.broadcasted_iota(jnp.int32, (b_rows, PAGE_SIZE), 1)
    a_new, r0 = _sb_tile(_dot(sq, sknt_ref[...].astype(BF16)), ki_b < qi_b, ntri_ref[:PAGE_SIZE, :PAGE_SIZE],
                         jnp.zeros((b_rows, 1), F32))
    sb0 = _dot_t(a_new.astype(BF16), svnt_ref[...].astype(BF16))

    def softmax_update(state, sc, values):
        m, l, acc = state
        m_new = jnp.maximum(m, jnp.max(sc, axis=1, keepdims=True))
        p = jnp.exp(sc - m_new)
        alpha = jnp.exp(m - m_new)
        return m_new, alpha * l + jnp.sum(p, axis=1, keepdims=True), alpha * acc + _dot(p.astype(BF16), values)

    n_blk = pps * PAGE_SIZE // MOBA_BLOCK
    chunk = lambda x, c: x[:, c * MOBA_BLOCK:(c + 1) * MOBA_BLOCK]

    def step(s, carry):
        r_run, sb_acc, mla, means, mblk, lblk = carry
        g = n * n_steps + s
        slot = g % 2
        for cp in page_copies(n, s, slot):
            cp.wait()
        last_of_seq = s + 1 == n_steps
        nxt_seq = jnp.minimum(jnp.where(last_of_seq, n + 1, n), n_seq - 1)
        nxt_step = jnp.where(last_of_seq, 0, s + 1)
        for cp in page_copies(nxt_seq, nxt_step, 1 - slot):
            cp.start()

        order = range(pps - 1, -1, -1)
        lanes_cat = lambda buf: jnp.concatenate([buf[slot, i] for i in order], axis=1)
        lat = lat_buf[slot].reshape(pps * PAGE_SIZE, MLA_KV_LORA).astype(BF16)
        krt = jnp.concatenate([kr_buf[slot, i] for i in range(pps)], axis=1).astype(BF16)
        skt = lanes_cat(sk_buf).astype(BF16)
        svt = lanes_cat(sv_buf).astype(BF16)
        mkt = lanes_cat(mk_buf)
        mvt = lanes_cat(mv_buf).astype(BF16)
        keys = pps * PAGE_SIZE
        blk0 = (n_pages - (s + 1) * pps) * PAGE_SIZE // MOBA_BLOCK

        c_sc = _dot_t(ql, lat) + _dot(qp, krt)
        b_z = _dot(sq, skt)
        a_sc = _dot(qxb, mkt.astype(BF16))

        sp = jnp.maximum(b_z, 0.0) + jnp.log(1.0 + jnp.exp(-jnp.abs(b_z)))
        sp_b = sp.astype(BF16)
        ntri = ntri_ref[...]
        tails = [_dot(chunk(sp_b, c), ntri) for c in range(n_blk)]
        sums = [jnp.sum(chunk(sp, c), axis=1, keepdims=True) for c in range(n_blk)]
        a_parts = [None] * n_blk
        for c in range(n_blk - 1, -1, -1):
            a_parts[c] = jnp.exp((chunk(b_z, c) - chunk(sp, c)) + tails[c] + r_run)
            r_run = r_run - sums[c]
        sb_acc = sb_acc + _dot_t(jnp.concatenate(a_parts, axis=1).astype(BF16), svt)

        mla = softmax_update(mla, c_sc, lat)

        kpos = (blk0 * MOBA_BLOCK + lax.broadcasted_iota(jnp.int32, (a_rows, keys), 1)).astype(F32)
        qpos = (past + lax.broadcasted_iota(jnp.int32, (a_rows, keys), 0) % nq).astype(F32)
        a_sc = a_sc - slope[:, :1] * (qpos - kpos)
        for c in range(n_blk):
            j = blk0 + c
            sc = chunk(a_sc, c)
            m = jnp.max(sc, axis=1, keepdims=True)
            p = jnp.exp(sc - m)
            mblk = jnp.where(a_lane == j, m, mblk)
            lblk = jnp.where(a_lane == j, jnp.sum(p, axis=1, keepdims=True), lblk)
            means = jnp.where(m_lane == j, jnp.sum(chunk(mkt, c), axis=1, keepdims=True) * (1.0 / MOBA_BLOCK), means)
            oblk_ref[j] = _dot_t(p.astype(BF16), chunk(mvt, c))
        return r_run, sb_acc, mla, means, mblk, lblk

    mla0 = (jnp.full((c_rows, 1), -jnp.inf, F32), jnp.zeros((c_rows, 1), F32), jnp.zeros((c_rows, MLA_KV_LORA), F32))
    init = (r0, sb0, mla0, jnp.zeros((width, LANES), F32), jnp.zeros((a_rows, LANES), F32), jnp.zeros((a_rows, LANES), F32))
    _, sb_acc, mla, means, mblk, lblk = lax.fori_loop(0, n_steps, step, init)
    ob_ref[...] = sb_acc

    @pl.when(n == n_seq - 1)
    def _drain():
        for cp in page_copies(n_seq - 1, 0, (n_seq * n_steps) % 2):
            cp.wait()

    cn = cn_ref[...].astype(BF16)
    qi_c = lax.broadcasted_iota(jnp.int32, (c_rows, PAGE_SIZE), 0) % nq
    ki_c = lax.broadcasted_iota(jnp.int32, (c_rows, PAGE_SIZE), 1)
    sc = _dot_t(ql, cn) + _dot(qp, krnt_ref[...].astype(BF16))
    sc = jnp.where(jnp.logical_and(ki_c <= qi_c, ki_c < nq), sc, -jnp.inf)
    _, l, acc = softmax_update(mla, sc, cn)
    full = _dot((acc / l).astype(BF16), wuv_ref[...])
    head = lax.broadcasted_iota(jnp.int32, (c_rows, MLA_V), 0) // nq
    out = jnp.zeros((c_rows, MLA_V), F32)
    for hh in range(MLA_HEADS):
        out = jnp.where(head == hh, full[:, hh * MLA_V:(hh + 1) * MLA_V], out)
    oc_ref[...] = out

    gate = _dot(qx, means, precision=HIGHEST)
    sel = _topk_mask(gate, a_lane < nb, a_lane.astype(F32), min(MOBA_TOPK, nb)) > 0.5
    qi_a = lax.broadcasted_iota(jnp.int32, (a_rows, PAGE_SIZE), 0) % nq
    ki_a = lax.broadcasted_iota(jnp.int32, (a_rows, PAGE_SIZE), 1)
    s_own = _dot(qxb, mknt_ref[...].astype(BF16)) - slope * (qi_a - ki_a).astype(F32)
    s_own = jnp.where(jnp.logical_and(qi_a >= ki_a, ki_a < nq), s_own, -jnp.inf)
    m_own = jnp.max(s_own, axis=1, keepdims=True)
    p_own = jnp.exp(s_own - m_own)
    l_own = jnp.sum(p_own, axis=1, keepdims=True)
    o_own = _dot_t(p_own.astype(BF16), mvnt_ref[...].astype(BF16))
    m_all = jnp.maximum(jnp.max(jnp.where(sel, mblk, -jnp.inf), axis=1, keepdims=True), m_own)
    w = jnp.where(sel, jnp.exp(mblk - m_all), 0.0)
    w_own = jnp.exp(m_own - m_all)
    l_all = jnp.sum(w * lblk, axis=1, keepdims=True) + w_own * l_own
    o_all = w_own * o_own
    for jj in range(nb):
        o_all = o_all + w[:, jj:jj + 1] * oblk_ref[jj]
    o_all = o_all / l_all
    first_kv = lax.broadcasted_iota(jnp.int32, (a_rows, HEAD_DIM), 0) < MOBA_GROUP * nq
    oa_ref[...] = jnp.where(first_kv, o_all[:, :HEAD_DIM], o_all[:, HEAD_DIM:])


def _sample_attn(moba_in, sb_in, mla_in, caches_t, page_table, layer, nq):
    qx, mknt, mvnt = moba_in
    sq, sknt, svnt = sb_in
    ql, qp, cn, krnt, wuv = mla_in
    n, n_pages = page_table.shape
    past = n_pages * PAGE_SIZE
    assert past % MOBA_BLOCK == 0 and PAGES_PER_STEP % 2 == 0 and n_pages % PAGES_PER_STEP == 0
    nb = past // MOBA_BLOCK
    assert nb <= LANES
    a_rows, b_rows, c_rows = MOBA_HEADS * nq, SB_HEADS * nq, MLA_HEADS * nq
    width = MOBA_KV_HEADS * HEAD_DIM
    i = jnp.arange(1, MOBA_HEADS + 1, dtype=F32)
    slope_rows = jnp.broadcast_to(jnp.repeat(jnp.exp2(-8.0 * i / MOBA_HEADS), nq)[:, None], (a_rows, LANES))

    def seq(shape):
        nd = len(shape)
        return pl.BlockSpec((None,) + shape, lambda s_, pt: (s_,) + (0,) * nd)

    def const(shape):
        nd = len(shape)
        return pl.BlockSpec(shape, lambda s_, pt: (0,) * nd)

    hbm = pl.BlockSpec(memory_space=pl.ANY)
    pps = PAGES_PER_STEP
    grid_spec = pltpu.PrefetchScalarGridSpec(
        num_scalar_prefetch=1, grid=(n,),
        in_specs=[seq((a_rows, width)), seq((width, PAGE_SIZE)), seq((width, PAGE_SIZE)), const((a_rows, LANES)),
                  seq((b_rows, HEAD_DIM)), seq((HEAD_DIM, PAGE_SIZE)), seq((HEAD_DIM, PAGE_SIZE)),
                  const((KEY_TILE, KEY_TILE)),
                  seq((c_rows, MLA_KV_LORA)), seq((c_rows, MLA_ROPE)), seq((PAGE_SIZE, MLA_KV_LORA)),
                  seq((MLA_ROPE, PAGE_SIZE)), const(wuv.shape)] + [hbm] * 6,
        out_specs=[seq((a_rows, HEAD_DIM)), seq((b_rows, HEAD_DIM)), seq((c_rows, MLA_V))],
        scratch_shapes=[pltpu.VMEM((2, pps, width, PAGE_SIZE), F32), pltpu.VMEM((2, pps, width, PAGE_SIZE), F32),
                        pltpu.VMEM((2, pps, HEAD_DIM, PAGE_SIZE), F32), pltpu.VMEM((2, pps, HEAD_DIM, PAGE_SIZE), F32),
                        pltpu.VMEM((2, pps, PAGE_SIZE, MLA_KV_LORA), F32), pltpu.VMEM((2, pps, MLA_ROPE, PAGE_SIZE), F32),
                        pltpu.SemaphoreType.DMA((6, 2)),
                        pltpu.VMEM((nb, a_rows, width), F32)])
    return pl.pallas_call(
        functools.partial(_sample_attn_kernel, layer=layer, nq=nq, n_pages=n_pages, nb=nb, past=past),
        grid_spec=grid_spec,
        out_shape=[jax.ShapeDtypeStruct((n, a_rows, HEAD_DIM), F32), jax.ShapeDtypeStruct((n, b_rows, HEAD_DIM), F32),
                   jax.ShapeDtypeStruct((n, c_rows, MLA_V), F32)],
        compiler_params=_params("arbitrary"), name="sample_attn",
    )(page_table, qx, mknt, mvnt, slope_rows, sq, sknt, svnt, _neg_tri(KEY_TILE), ql, qp, cn, krnt, wuv, *caches_t)


def _layer_weights(l, p):
    w_in = p['w_in'][l]
    wa = w_in[:, :HI_COLS]
    wa_hi = wa.astype(BF16)
    half = MLA_ROPE // 2
    kpe0 = w_in.shape[1] - MLA_ROPE
    kpe_sw = jnp.concatenate([w_in[:, kpe0 + half:], w_in[:, kpe0:kpe0 + half]], axis=1)
    wq = p['mla_w_uq'][l].reshape(MLA_Q_LORA, MLA_HEADS, MLA_QK).transpose(1, 0, 2)
    wq_sw = jnp.concatenate([wq[..., :MLA_NOPE], wq[..., MLA_NOPE + half:], wq[..., MLA_NOPE:MLA_NOPE + half]], axis=-1)
    w_uk = p['mla_w_uk'][l].transpose(1, 0, 2)
    w_uv = p['mla_w_uv'][l]
    row = lambda a: a.reshape(1, -1)
    return dict(
        g1=row(p['norm1_g'][l]), wa_hi=wa_hi, wa_lo=(wa - wa_hi.astype(F32)).astype(BF16),
        wb=jnp.concatenate([w_in[:, HI_COLS:], kpe_sw], axis=1).astype(BF16),
        qn_g=row(p['mla_q_norm_g'][l]), wq6=wq.astype(BF16), wq6s=wq_sw.astype(BF16), kvn_g=row(p['mla_kv_norm_g'][l]),
        wk6=jnp.pad(w_uk, ((0, 0), (0, 0), (0, MLA_ROPE))).astype(BF16),
        ekr=jnp.concatenate([jnp.zeros((MLA_ROPE, MLA_NOPE), F32), jnp.eye(MLA_ROPE, dtype=F32)], axis=1).astype(BF16),
        wv6t=w_uv.transpose(1, 2, 0).astype(BF16),
        wukt6=w_uk.transpose(0, 2, 1).astype(BF16),
        wuv_flat=w_uv.reshape(MLA_KV_LORA, MLA_HEADS * MLA_V).astype(BF16),
        g_out=row(p['out_norm_g'][l]), w_out=p['w_out'][l].astype(BF16), g2=row(p['norm2_g'][l]),
        wg=p['ffn_w_gate'][l].astype(BF16), wu=p['ffn_w_up'][l].astype(BF16), cw=p['ffn_conv_w'][l],
        cb=row(p['ffn_conv_b'][l]), wd=p['ffn_w_down'][l].astype(BF16), g_final=row(p['final_norm_g']))


def _rope_tables(pos):
    half = MLA_ROPE // 2
    inv = ROPE_THETA ** (-jnp.arange(half, dtype=F32) / half)
    ang = pos.astype(F32)[:, None] * inv[None, :]
    cos, sin = jnp.cos(ang), jnp.sin(ang)
    n = pos.shape[0]
    rc = jnp.concatenate([jnp.ones((n, MLA_NOPE), F32), cos, cos], axis=1)
    rs = jnp.concatenate([jnp.zeros((n, MLA_NOPE), F32), -sin, sin], axis=1)
    return rc, rs


def _prompt_forward(x, weights, depth):
    b, s, d = x.shape
    assert s % MOBA_BLOCK == 0
    t = b * s
    tm = KEY_TILE
    rc, rs = _rope_tables(jnp.arange(s))
    rc, rs = jnp.tile(rc, (b, 1)), jnp.tile(rs, (b, 1))
    x2d = x.reshape(t, d)
    st = [[] for _ in range(7)]
    for l in range(depth):
        w = weights[l]
        (qa6, ka, ka2, va, va2, qb4, kb, vb, ckv, kpe, mq6, mk6, mv6) = _project(x2d, rc, rs, w, False, tm)
        oa6 = _moba_prompt(qa6, ka2, va2, b, s)
        ob4 = _sb_prompt(qb4, kb, vb, b, s)
        oc6 = _mla_prompt(mq6, mk6, mv6, b, s)
        x2d, gl = _merge_ffn(x2d, oa6, ob4, oc6, w, l == depth - 1, tm, tiles_per_seq=s // tm)
        buf = gl.reshape(b, s // tm, 8, -1)[:, -1, 8 - (CONV_W - 1):, :]
        for lst, a in zip(st, (ka.reshape(b, s, MOBA_KV_HEADS, HEAD_DIM), va.reshape(b, s, MOBA_KV_HEADS, HEAD_DIM),
                               kb.reshape(b, s, HEAD_DIM), vb.reshape(b, s, HEAD_DIM),
                               ckv.reshape(b, s, MLA_KV_LORA), kpe.reshape(b, s, MLA_ROPE), buf)):
            lst.append(a)
    return x2d.reshape(b, s, d), [jnp.stack(a, axis=0) for a in st]


def _pad_tokens_t(a, n, nq):
    a = a.reshape(n, nq, -1).transpose(0, 2, 1)
    return jnp.pad(a, ((0, 0), (0, 0), (0, PAGE_SIZE - nq)))


def _rows_per_seq(a, n, nq):
    h = a.shape[0]
    return a.reshape(h, n, nq, -1).transpose(1, 0, 2, 3).reshape(n, h * nq, -1)


def _heads_major(a, n, nq, h):
    return a.reshape(n, h, nq, -1).transpose(1, 0, 2, 3).reshape(h, n * nq, -1)


def _sample_forward(x, caches, s_conv, page_table, weights, depth):
    n, nq, d = x.shape
    assert nq <= 8 and CONV_W - 1 <= nq
    c_mk, c_mv, c_sk, c_sv, c_mc, c_mr = caches
    n_pages = page_table.shape[1]
    past = n_pages * PAGE_SIZE
    t = n * nq
    tm = min(256, t)
    rc, rs = _rope_tables(past + jnp.arange(nq))
    rc, rs = jnp.tile(rc, (n, 1)), jnp.tile(rs, (n, 1))
    width = MOBA_KV_HEADS * HEAD_DIM
    mk_t = c_mk.transpose(0, 1, 3, 4, 2).reshape(c_mk.shape[0], c_mk.shape[1], width, PAGE_SIZE)
    mv_t = c_mv.transpose(0, 1, 3, 4, 2).reshape(c_mv.shape[0], c_mv.shape[1], width, PAGE_SIZE)
    sk_t = c_sk.transpose(0, 1, 3, 2)
    sv_t = c_sv.transpose(0, 1, 3, 2)
    mr_t = c_mr.transpose(0, 1, 3, 2)
    x2d = x.reshape(t, d)
    st = [[] for _ in range(7)]
    for l in range(depth):
        w = weights[l]
        (qa6, ka, va, qb4, kb, vb, ckv, kpe, qlat6, qpe6) = _project(x2d, rc, rs, w, True, tm)
        qa = _rows_per_seq(qa6, n, nq)
        kvh = (jnp.arange(MOBA_HEADS * nq) // (MOBA_GROUP * nq))[None, :, None]
        qx = jnp.concatenate([jnp.where(kvh == 0, qa, 0.0), jnp.where(kvh == 1, qa, 0.0)], axis=-1)
        cn = jnp.pad(ckv.reshape(n, nq, MLA_KV_LORA), ((0, 0), (0, PAGE_SIZE - nq), (0, 0)))
        oa, ob, oc = _sample_attn(
            (qx, _pad_tokens_t(ka, n, nq), _pad_tokens_t(va, n, nq)),
            (_rows_per_seq(qb4, n, nq), _pad_tokens_t(kb, n, nq), _pad_tokens_t(vb, n, nq)),
            (_rows_per_seq(qlat6, n, nq), _rows_per_seq(qpe6, n, nq), cn, _pad_tokens_t(kpe, n, nq), w['wuv_flat']),
            (mk_t, mv_t, sk_t, sv_t, c_mc, mr_t), page_table, l, nq)
        buf = s_conv[l]
        zero = jnp.zeros((n, nq - 1, buf.shape[-1]), F32)
        st1 = jnp.concatenate([buf[:, 1:2], zero], axis=1).reshape(t, -1)
        st2 = jnp.concatenate([buf, zero[:, :nq - 2]], axis=1).reshape(t, -1)
        x2d, g = _merge_ffn(x2d, _heads_major(oa, n, nq, MOBA_HEADS), _heads_major(ob, n, nq, SB_HEADS),
                            _heads_major(oc, n, nq, MLA_HEADS), w, l == depth - 1, tm, seq_rows=nq, st=(st1, st2))
        new_buf = g.reshape(n, nq, -1)[:, nq - (CONV_W - 1):, :]
        for lst, a in zip(st, (ka.reshape(n, nq, MOBA_KV_HEADS, HEAD_DIM), va.reshape(n, nq, MOBA_KV_HEADS, HEAD_DIM),
                               kb.reshape(n, nq, HEAD_DIM), vb.reshape(n, nq, HEAD_DIM),
                               ckv.reshape(n, nq, MLA_KV_LORA), kpe.reshape(n, nq, MLA_ROPE), new_buf)):
            lst.append(a)
    return x2d.reshape(n, nq, d), [jnp.stack(a, axis=0) for a in st]


def kernel(x_prompt, x_sample, cache_moba_k, cache_moba_v, cache_sb_k, cache_sb_v, cache_mla_latent, cache_mla_krope, state_ffn_conv, page_table, norm1_g, w_in, mla_q_norm_g, mla_w_uq, mla_kv_norm_g, mla_w_uk, mla_w_uv, out_norm_g, w_out, norm2_g, ffn_w_gate, ffn_w_up, ffn_conv_w, ffn_conv_b, ffn_w_down, final_norm_g):
    p = dict(norm1_g=norm1_g, w_in=w_in, mla_q_norm_g=mla_q_norm_g, mla_w_uq=mla_w_uq,
             mla_kv_norm_g=mla_kv_norm_g, mla_w_uk=mla_w_uk, mla_w_uv=mla_w_uv, out_norm_g=out_norm_g,
             w_out=w_out, norm2_g=norm2_g, ffn_w_gate=ffn_w_gate, ffn_w_up=ffn_w_up,
             ffn_conv_w=ffn_conv_w, ffn_conv_b=ffn_conv_b, ffn_w_down=ffn_w_down, final_norm_g=final_norm_g)
    depth = w_in.shape[0]
    weights = [_layer_weights(l, p) for l in range(depth)]
    y_prompt, pst = _prompt_forward(x_prompt, weights, depth)
    y_sample, sst = _sample_forward(
        x_sample, (cache_moba_k, cache_moba_v, cache_sb_k, cache_sb_v, cache_mla_latent, cache_mla_krope),
        state_ffn_conv, page_table, weights, depth)
    return (y_prompt, y_sample, *pst, *sst)
```

```python
import functools

import jax
import jax.numpy as jnp
from jax import lax
from jax.experimental import pallas as pl
from jax.experimental.pallas import tpu as pltpu

F32 = jnp.float32
BF16 = jnp.bfloat16
HIGHEST = lax.Precision.HIGHEST

HEAD_DIM = 64
MOBA_HEADS = 6
MOBA_KV_HEADS = 2
MOBA_GROUP = MOBA_HEADS // MOBA_KV_HEADS
MOBA_BLOCK = 256
MOBA_TOPK = 3
SB_HEADS = 4
MLA_HEADS = 6
MLA_Q_LORA = 256
MLA_KV_LORA = 256
MLA_NOPE = 64
MLA_ROPE = 32
MLA_QK = MLA_NOPE + MLA_ROPE
MLA_V = 64
ROPE_THETA = 10000.0
MIX_A = MOBA_HEADS * HEAD_DIM
MIX_B = SB_HEADS * HEAD_DIM
MIX_C = MLA_HEADS * MLA_V
CONV_W = 3
RMS_EPS = 1e-6
PAGE_SIZE = 128

V7X_VMEM_BYTES = 64 * 1024 * 1024
VMEM_LIMIT = (V7X_VMEM_BYTES * 7) // 8
LANES = 128

KEY_TILE = 256
MASK_NEG = -1e30
PAGES_PER_STEP = 16
HI_COLS = MIX_A + MOBA_KV_HEADS * HEAD_DIM


def _dot(a, b, precision=None):
    return jnp.dot(a, b, preferred_element_type=F32, precision=precision)


def _dot_t(a, b, precision=None):
    return lax.dot_general(a, b, (((1,), (1,)), ((), ())), preferred_element_type=F32, precision=precision)


def _rms(x, g):
    return x * lax.rsqrt(jnp.mean(x * x, axis=-1, keepdims=True) + RMS_EPS) * g


def _params(*sem):
    return pltpu.CompilerParams(dimension_semantics=sem, vmem_limit_bytes=VMEM_LIMIT)


def _const_spec(shape):
    nd = len(shape)
    return pl.BlockSpec(shape, lambda *_: (0,) * nd)


def _proj_kernel(*refs, absorbed):
    (x_ref, g1_ref, wah_ref, wal_ref, wb_ref, qng_ref, wq_ref, wqs_ref, kvg_ref, rc_ref, rs_ref) = refs[:11]
    if absorbed:
        (wukt_ref, qa_ref, ka_ref, va_ref, qb_ref, kb_ref, vb_ref, ckv_ref, kpe_ref, qlat_ref, qpe_ref) = refs[11:]
    else:
        (wk_ref, ekr_ref, wv_ref, qa_ref, ka_ref, ka2_ref, va_ref, va2_ref, qb_ref, kb_ref, vb_ref, ckv_ref, kpe_ref,
         mq_ref, mk_ref, mv_ref) = refs[11:]
    h = _rms(x_ref[...], g1_ref[...])
    h_hi = h.astype(BF16)
    h_lo = (h - h_hi.astype(F32)).astype(BF16)
    pa = _dot(h_hi, wah_ref[...]) + (_dot(h_lo, wah_ref[...]) + _dot(h_hi, wal_ref[...]))
    pb = _dot(h_hi, wb_ref[...])
    for k in range(MOBA_HEADS):
        qa_ref[k] = pa[:, k * HEAD_DIM:(k + 1) * HEAD_DIM] * (HEAD_DIM ** -0.5)
    ka = pa[:, MIX_A:MIX_A + 2 * HEAD_DIM]
    ka_ref[...] = ka
    va = pb[:, 0:128]
    va_ref[...] = va
    if not absorbed:
        for k in range(MOBA_KV_HEADS):
            ka2_ref[k] = ka[:, k * HEAD_DIM:(k + 1) * HEAD_DIM]
        va2_ref[0] = va.T.astype(BF16)
    for k in range(SB_HEADS):
        qb_ref[k] = (pb[:, 128 + k * HEAD_DIM:128 + (k + 1) * HEAD_DIM] * (HEAD_DIM ** -0.5)).astype(BF16)
    kb_ref[...] = pb[:, 384:448]
    vb_ref[...] = pb[:, 448:512]
    cqn = _rms(pb[:, 512:768], qng_ref[...]).astype(BF16)
    ckvn = _rms(pb[:, 768:1024], kvg_ref[...])
    ckv_ref[...] = ckvn
    rc = rc_ref[...]
    rs = rs_ref[...]
    kr = pb[:, 1024:1056] * rc[:, MLA_NOPE:] + pb[:, 1056:1088] * rs[:, MLA_NOPE:]
    kpe_ref[...] = kr
    scale = MLA_QK ** -0.5
    if not absorbed:
        ckvn_b = ckvn.astype(BF16)
        kr_b = kr.astype(BF16)
    for hh in range(MLA_HEADS):
        qh = (_dot(cqn, wq_ref[hh]) * rc + _dot(cqn, wqs_ref[hh]) * rs) * scale
        if absorbed:
            qlat_ref[hh] = _dot(qh[:, :MLA_NOPE].astype(BF16), wukt_ref[hh])
            qpe_ref[hh] = qh[:, MLA_NOPE:]
        else:
            mq_ref[hh] = qh.astype(BF16)
            mk_ref[hh] = (_dot(ckvn_b, wk_ref[hh]) + _dot(kr_b, ekr_ref[...])).astype(BF16)
            mv_ref[hh, 0] = _dot_t(wv_ref[hh], ckvn_b).astype(BF16)


def _project(x2d, rc, rs, w, absorbed, tm):
    t, d = x2d.shape
    grid = (t // tm,)
    rows = lambda width: pl.BlockSpec((tm, width), lambda i: (i, 0))
    heads = lambda nh, width: pl.BlockSpec((nh, tm, width), lambda i: (0, i, 0))
    ins = [x2d, w['g1'], w['wa_hi'], w['wa_lo'], w['wb'], w['qn_g'], w['wq6'], w['wq6s'], w['kvn_g'], rc, rs]
    in_specs = [rows(d)] + [_const_spec(a.shape) for a in ins[1:9]] + [rows(MLA_QK), rows(MLA_QK)]
    if absorbed:
        extra = [w['wukt6']]
        outs = [((MOBA_HEADS, t, HEAD_DIM), F32, heads(MOBA_HEADS, HEAD_DIM)),
                ((t, 128), F32, rows(128)), ((t, 128), F32, rows(128)),
                ((SB_HEADS, t, HEAD_DIM), BF16, heads(SB_HEADS, HEAD_DIM)),
                ((t, HEAD_DIM), F32, rows(HEAD_DIM)), ((t, HEAD_DIM), F32, rows(HEAD_DIM)),
                ((t, MLA_KV_LORA), F32, rows(MLA_KV_LORA)), ((t, MLA_ROPE), F32, rows(MLA_ROPE)),
                ((MLA_HEADS, t, MLA_KV_LORA), F32, heads(MLA_HEADS, MLA_KV_LORA)),
                ((MLA_HEADS, t, MLA_ROPE), F32, heads(MLA_HEADS, MLA_ROPE))]
    else:
        extra = [w['wk6'], w['ekr'], w['wv6t']]
        outs = [((MOBA_HEADS, t, HEAD_DIM), F32, heads(MOBA_HEADS, HEAD_DIM)),
                ((t, 128), F32, rows(128)), ((MOBA_KV_HEADS, t, HEAD_DIM), F32, heads(MOBA_KV_HEADS, HEAD_DIM)),
                ((t, 128), F32, rows(128)),
                ((t // tm, 128, tm), BF16, pl.BlockSpec((1, 128, tm), lambda i: (i, 0, 0))),
                ((SB_HEADS, t, HEAD_DIM), BF16, heads(SB_HEADS, HEAD_DIM)),
                ((t, HEAD_DIM), F32, rows(HEAD_DIM)), ((t, HEAD_DIM), F32, rows(HEAD_DIM)),
                ((t, MLA_KV_LORA), F32, rows(MLA_KV_LORA)), ((t, MLA_ROPE), F32, rows(MLA_ROPE)),
                ((MLA_HEADS, t, MLA_QK), BF16, heads(MLA_HEADS, MLA_QK)),
                ((MLA_HEADS, t, MLA_QK), BF16, heads(MLA_HEADS, MLA_QK)),
                ((MLA_HEADS, t // tm, MLA_V, tm), BF16,
                 pl.BlockSpec((MLA_HEADS, 1, MLA_V, tm), lambda i: (0, i, 0, 0)))]
    ins += extra
    in_specs += [_const_spec(a.shape) for a in extra]
    return pl.pallas_call(
        functools.partial(_proj_kernel, absorbed=absorbed),
        grid=grid, in_specs=in_specs,
        out_specs=[o[2] for o in outs],
        out_shape=[jax.ShapeDtypeStruct(o[0], o[1]) for o in outs],
        compiler_params=_params("parallel"),
        name="proj_sample" if absorbed else "proj_prompt",
    )(*ins)


def _topk_mask(gate, allowed, col_f, k_sel):
    g = jnp.where(allowed, gate, -jnp.inf)
    sel = jnp.zeros(gate.shape, F32)
    for _ in range(k_sel):
        m = jnp.max(g, axis=1, keepdims=True)
        cand = jnp.where(g == m, col_f, 1e9)
        cand = jnp.where(m > -jnp.inf, cand, 1e9)
        first = jnp.min(cand, axis=1, keepdims=True)
        pick = col_f == first
        sel = jnp.where(pick, 1.0, sel)
        g = jnp.where(pick, -jnp.inf, g)
    return sel


def _moba_prompt_kernel(q_ref, k_ref, vt_ref, slope_ref, d0_ref, o_ref, means_ref, kext_ref, *, tq, nb, ext):
    c = pl.program_id(1)
    rows = MOBA_GROUP * tq
    kv_heads = range(MOBA_KV_HEADS)

    @pl.when(c == 0)
    def _build():
        means_ref[...] = jnp.zeros(means_ref.shape, F32)
        col = lax.broadcasted_iota(jnp.int32, (MOBA_BLOCK, ext), 1)

        def body(j, carry):
            r0 = pl.multiple_of(j * MOBA_BLOCK, MOBA_BLOCK)
            onehot = jnp.where(col == j, 1.0, 0.0)
            for hk in kv_heads:
                kb = k_ref[hk, pl.ds(r0, MOBA_BLOCK), :]
                means_ref[hk, pl.ds(j, 1), :] = jnp.sum(kb, axis=0, keepdims=True) * (1.0 / MOBA_BLOCK)
                kext_ref[hk, pl.ds(r0, MOBA_BLOCK), :] = jnp.concatenate([kb, onehot], axis=1).astype(BF16)
            return carry

        lax.fori_loop(0, nb, body, 0)

    n_past = (c * tq) // MOBA_BLOCK
    col_i = lax.broadcasted_iota(jnp.int32, (rows, ext), 1)
    q_ext = []
    for hk in kv_heads:
        q = q_ref[hk * MOBA_GROUP:(hk + 1) * MOBA_GROUP].reshape(rows, HEAD_DIM)
        gate = _dot_t(q, means_ref[hk], precision=HIGHEST)
        sel = _topk_mask(gate, col_i < n_past, col_i.astype(F32), min(MOBA_TOPK, nb))
        keep = jnp.logical_or(sel > 0.5, col_i == n_past)
        q_ext.append(jnp.concatenate([q, jnp.where(keep, 0.0, MASK_NEG)], axis=1).astype(BF16))
    d0 = d0_ref[...]

    def blocks(js, carry, causal):
        r0s = [pl.multiple_of(j * MOBA_BLOCK, MOBA_BLOCK) for j in js]
        ds = [d0 + (c * tq - j * MOBA_BLOCK).astype(F32) for j in js]
        ss = [[_dot_t(kext_ref[hk, pl.ds(r0, MOBA_BLOCK), :], q_ext[hk]) for r0 in r0s] for hk in kv_heads]
        mid = []
        for hk in kv_heads:
            m, l, acc = carry[hk]
            s_list = []
            for s, d in zip(ss[hk], ds):
                s = s - slope_ref[hk] * d
                s_list.append(jnp.where(d >= 0, s, -jnp.inf) if causal else s)
            m_new = m
            for s in s_list:
                m_new = jnp.maximum(m_new, jnp.max(s, axis=0, keepdims=True))
            alpha = jnp.exp(m - m_new)
            l = alpha * l
            p_list = []
            for s in s_list:
                p = jnp.exp(s - m_new)
                l = l + jnp.sum(p, axis=0, keepdims=True)
                p_list.append(p.astype(BF16))
            mid.append((m_new, l, alpha * acc, p_list))
        out = []
        for hk, (m_new, l, acc, p_list) in enumerate(mid):
            for j, p in zip(js, p_list):
                acc = acc + _dot(vt_ref[j, hk * HEAD_DIM:(hk + 1) * HEAD_DIM, :], p)
            out.append((m_new, l, acc))
        return tuple(out)

    init = tuple((jnp.full((1, rows), -jnp.inf, F32), jnp.zeros((1, rows), F32), jnp.zeros((HEAD_DIM, rows), F32))
                 for _ in kv_heads)
    carry = lax.fori_loop(0, n_past // 2, lambda i, cr: blocks([2 * i, 2 * i + 1], cr, False), init)
    carry = lax.fori_loop(0, n_past % 2, lambda i, cr: blocks([n_past - 1], cr, False), carry)
    carry = blocks([n_past], carry, True)
    heads = []
    for hk in kv_heads:
        _, l, acc = carry[hk]
        o = acc / l
        heads += [o[:, g * tq:(g + 1) * tq] for g in range(MOBA_GROUP)]
    for pair in range(MOBA_HEADS // 2):
        o_ref[pair] = jnp.concatenate(heads[2 * pair:2 * pair + 2], axis=0).T


def _moba_prompt(qa6, ka2, vat, b, s, tq=128):
    nq = s // tq
    nb = s // MOBA_BLOCK
    ext = max(32, -(-nb // 32) * 32)
    rows = MOBA_GROUP * tq
    i = jnp.arange(1, MOBA_HEADS + 1, dtype=F32)
    slopes = jnp.exp2(-8.0 * i / MOBA_HEADS).reshape(MOBA_KV_HEADS, MOBA_GROUP)
    slope_full = jnp.broadcast_to(jnp.repeat(slopes, tq, axis=1)[:, None, :], (MOBA_KV_HEADS, MOBA_BLOCK, rows))
    qi = jnp.tile(jnp.arange(tq, dtype=F32), MOBA_GROUP)
    d0 = qi[None, :] - jnp.arange(MOBA_BLOCK, dtype=F32)[:, None]
    once = pl.Buffered(1)
    return pl.pallas_call(
        functools.partial(_moba_prompt_kernel, tq=tq, nb=nb, ext=ext),
        grid=(b, nq),
        in_specs=[pl.BlockSpec((MOBA_HEADS, tq, HEAD_DIM), lambda bi, ci: (0, bi * nq + ci, 0)),
                  pl.BlockSpec((MOBA_KV_HEADS, s, HEAD_DIM), lambda bi, ci: (0, bi, 0), pipeline_mode=once),
                  pl.BlockSpec((nb, MOBA_KV_HEADS * HEAD_DIM, MOBA_BLOCK), lambda bi, ci: (bi, 0, 0),
                               pipeline_mode=once),
                  _const_spec((MOBA_KV_HEADS, MOBA_BLOCK, rows)),
                  _const_spec((MOBA_BLOCK, rows))],
        out_specs=pl.BlockSpec((MOBA_HEADS // 2, tq, 2 * HEAD_DIM), lambda bi, ci: (0, bi * nq + ci, 0)),
        out_shape=jax.ShapeDtypeStruct((MOBA_HEADS // 2, b * s, 2 * HEAD_DIM), F32),
        scratch_shapes=[pltpu.VMEM((MOBA_KV_HEADS, ext, HEAD_DIM), F32),
                        pltpu.VMEM((MOBA_KV_HEADS, s, HEAD_DIM + ext), BF16)],
        compiler_params=_params("parallel", "arbitrary"),
        name="moba_prompt",
    )(qa6, ka2, vat, slope_full, d0)


def _sb_tile(z, mask, neg_tri, r_run):
    sp = jnp.maximum(z, 0.0) + jnp.log(1.0 + jnp.exp(-jnp.abs(z)))
    if mask is not None:
        sp = jnp.where(mask, sp, 0.0)
    tail = _dot(sp.astype(BF16), neg_tri)
    a = jnp.exp((z - sp) + tail + r_run)
    if mask is not None:
        a = jnp.where(mask, a, 0.0)
    return a, r_run - jnp.sum(sp, axis=1, keepdims=True)


def _sb_prompt_kernel(q_ref, k_ref, v_ref, tri_ref, o_ref, kb_ref, vb_ref, *, tq, n_chains):
    c = pl.program_id(1)

    @pl.when(c == 0)
    def _cast():
        kb_ref[...] = k_ref[...].astype(BF16)
        vb_ref[...] = v_ref[...].astype(BF16)

    hpc = SB_HEADS // n_chains
    rows = hpc * tq
    qs = [q_ref[ch * hpc:(ch + 1) * hpc].reshape(rows, HEAD_DIM) for ch in range(n_chains)]
    tri = tri_ref[...]
    jd = (c * tq) // KEY_TILE

    def tiles(js, carry, masked):
        r0s = [pl.multiple_of(j * KEY_TILE, KEY_TILE) for j in js]
        kts = [kb_ref[pl.ds(r0, KEY_TILE), :] for r0 in r0s]
        vts = [vb_ref[pl.ds(r0, KEY_TILE), :] for r0 in r0s]
        mask = None
        if masked:
            qpos = c * tq + (lax.broadcasted_iota(jnp.int32, (rows, KEY_TILE), 0) & (tq - 1))
            kpos = js[0] * KEY_TILE + lax.broadcasted_iota(jnp.int32, (rows, KEY_TILE), 1)
            mask = kpos < qpos
        zs = [[_dot_t(qs[ch], kt) for kt in kts] for ch in range(n_chains)]
        sps = []
        for ch in range(n_chains):
            row = []
            for z in zs[ch]:
                sp = jnp.maximum(z, 0.0) + jnp.log(1.0 + jnp.exp(-jnp.abs(z)))
                row.append(jnp.where(mask, sp, 0.0) if masked else sp)
            sps.append(row)
        tails = [[_dot(sp.astype(BF16), tri) for sp in sps[ch]] for ch in range(n_chains)]
        mid = []
        for ch in range(n_chains):
            r_run, acc = carry[ch]
            a_list = []
            for t in range(len(js)):
                a = jnp.exp((zs[ch][t] - sps[ch][t]) + tails[ch][t] + r_run)
                a_list.append((jnp.where(mask, a, 0.0) if masked else a).astype(BF16))
                r_run = r_run - jnp.sum(sps[ch][t], axis=1, keepdims=True)
            mid.append((r_run, acc, a_list))
        out = []
        for r_run, acc, a_list in mid:
            for a, vt in zip(a_list, vts):
                acc = acc + _dot(a, vt)
            out.append((r_run, acc))
        return tuple(out)

    init = tuple((jnp.zeros((rows, 1), F32), jnp.zeros((rows, HEAD_DIM), F32)) for _ in range(n_chains))
    carry = tiles([jd], init, True)
    carry = lax.fori_loop(0, jd // 2, lambda i, cr: tiles([jd - 1 - 2 * i, jd - 2 - 2 * i], cr, False), carry)
    carry = lax.fori_loop(0, jd % 2, lambda i, cr: tiles([0], cr, False), carry)
    for ch in range(n_chains):
        o_ref[ch * hpc:(ch + 1) * hpc] = carry[ch][1].reshape(hpc, tq, HEAD_DIM)


def _neg_tri(n):
    r = jnp.arange(n)
    return -((r[:, None] > r[None, :]).astype(BF16))


def _sb_prompt(qb4, kb, vb, b, s, tq=128, n_chains=2):
    nq = s // tq
    return pl.pallas_call(
        functools.partial(_sb_prompt_kernel, tq=tq, n_chains=n_chains),
        grid=(b, nq),
        in_specs=[pl.BlockSpec((SB_HEADS, tq, HEAD_DIM), lambda bi, ci: (0, bi * nq + ci, 0)),
                  pl.BlockSpec((s, HEAD_DIM), lambda bi, ci: (bi, 0)),
                  pl.BlockSpec((s, HEAD_DIM), lambda bi, ci: (bi, 0)),
                  _const_spec((KEY_TILE, KEY_TILE))],
        out_specs=pl.BlockSpec((SB_HEADS, tq, HEAD_DIM), lambda bi, ci: (0, bi * nq + ci, 0)),
        out_shape=jax.ShapeDtypeStruct((SB_HEADS, b * s, HEAD_DIM), F32),
        scratch_shapes=[pltpu.VMEM((s, HEAD_DIM), BF16), pltpu.VMEM((s, HEAD_DIM), BF16)],
        compiler_params=_params("parallel", "arbitrary"),
        name="sb_prompt",
    )(qb4, kb, vb, _neg_tri(KEY_TILE))


def _mla_prompt_kernel(q_ref, k_ref, vt_ref, o_ref, *, tq):
    c = pl.program_id(1)
    nh = MLA_HEADS
    qs = [q_ref[h] for h in range(nh)]
    jd = (c * tq) // KEY_TILE

    def tiles(js, carry, masked):
        r0s = [pl.multiple_of(j * KEY_TILE, KEY_TILE) for j in js]
        if masked:
            kpos = js[0] * KEY_TILE + lax.broadcasted_iota(jnp.int32, (KEY_TILE, tq), 0)
            qpos = c * tq + lax.broadcasted_iota(jnp.int32, (KEY_TILE, tq), 1)
            visible = kpos <= qpos
        ss = [[_dot_t(k_ref[h, pl.ds(r0, KEY_TILE), :], qs[h]) for r0 in r0s] for h in range(nh)]
        mid = []
        for h in range(nh):
            m, l, acc = carry[h]
            s_list = [jnp.where(visible, s, -jnp.inf) for s in ss[h]] if masked else ss[h]
            m_new = m
            for s in s_list:
                m_new = jnp.maximum(m_new, jnp.max(s, axis=0, keepdims=True))
            alpha = jnp.exp(m - m_new)
            l = alpha * l
            p_list = []
            for s in s_list:
                p = jnp.exp(s - m_new)
                l = l + jnp.sum(p, axis=0, keepdims=True)
                p_list.append(p.astype(BF16))
            mid.append((m_new, l, alpha * acc, p_list))
        out = []
        for h, (m_new, l, acc, p_list) in enumerate(mid):
            for j, p in zip(js, p_list):
                acc = acc + _dot(vt_ref[h, j], p)
            out.append((m_new, l, acc))
        return tuple(out)

    init = tuple((jnp.full((1, tq), -jnp.inf, F32), jnp.zeros((1, tq), F32), jnp.zeros((MLA_V, tq), F32))
                 for _ in range(nh))
    carry = tiles([jd], init, True)
    carry = lax.fori_loop(0, jd // 2, lambda i, cr: tiles([2 * i, 2 * i + 1], cr, False), carry)
    carry = lax.fori_loop(0, jd % 2, lambda i, cr: tiles([jd - 1], cr, False), carry)
    for pair in range(nh // 2):
        outs = [carry[2 * pair + e][2] / carry[2 * pair + e][1] for e in range(2)]
        o_ref[pair] = jnp.concatenate(outs, axis=0).T


def _mla_prompt(mq6, mk6, mvt6, b, s):
    tq = KEY_TILE
    nq = s // tq
    once = pl.Buffered(1)
    return pl.pallas_call(
        functools.partial(_mla_prompt_kernel, tq=tq),
        grid=(b, nq),
        in_specs=[pl.BlockSpec((MLA_HEADS, tq, MLA_QK), lambda bi, ci: (0, bi * nq + ci, 0)),
                  pl.BlockSpec((MLA_HEADS, s, MLA_QK), lambda bi, ci: (0, bi, 0), pipeline_mode=once),
                  pl.BlockSpec((MLA_HEADS, nq, MLA_V, KEY_TILE), lambda bi, ci: (0, bi, 0, 0), pipeline_mode=once)],
        out_specs=pl.BlockSpec((MLA_HEADS // 2, tq, 2 * MLA_V), lambda bi, ci: (0, bi * nq + ci, 0)),
        out_shape=jax.ShapeDtypeStruct((MLA_HEADS // 2, b * s, 2 * MLA_V), F32),
        compiler_params=_params("parallel", "arbitrary"),
        name="mla_prompt",
    )(mq6, mk6, mvt6)


def _merge_ffn_kernel(*refs, tm, seq_rows, tiles_per_seq, n_chunks, final):
    (x_ref, oa_ref, ob_ref, oc_ref, gout_ref, wout_ref, g2_ref, wg_ref, wu_ref, cw_ref, cb_ref, wd_ref, gf_ref) = refs[:13]
    carry_mode = seq_rows is None
    if carry_mode:
        y_ref, gl_ref, gtail_ref = refs[13:]
    else:
        st1_ref, st2_ref, y_ref, gl_ref = refs[13:]
    i = pl.program_id(0)
    if carry_mode:
        @pl.when(i == 0)
        def _init():
            gtail_ref[...] = jnp.zeros(gtail_ref.shape, F32)

    def group(ref):
        parts = [ref[k] for k in range(ref.shape[0])]
        ssq = parts[0] * parts[0]
        for p in parts[1:]:
            ssq = ssq + p * p
        r = lax.rsqrt(jnp.sum(ssq, axis=-1, keepdims=True) * (1.0 / (ref.shape[0] * ref.shape[2])) + RMS_EPS)
        return [p * r for p in parts]

    o = jnp.concatenate(group(oa_ref) + group(ob_ref) + group(oc_ref), axis=-1)
    x1 = x_ref[...] + _dot((o * gout_ref[...]).astype(BF16), wout_ref[...])
    h2 = _rms(x1, g2_ref[...]).astype(BF16)
    d_ff = wg_ref.shape[1]
    fc = d_ff // n_chunks
    row = lax.broadcasted_iota(jnp.int32, (tm, 1), 0)
    y = x1
    for ci in range(n_chunks):
        sl = slice(ci * fc, (ci + 1) * fc)
        g = _dot(h2, wg_ref[:, sl])
        u = _dot(h2, wu_ref[:, sl])
        if carry_mode:
            prev = jnp.where(i % tiles_per_seq == 0, 0.0, gtail_ref[:, sl])
            p6, p7 = prev[6:7, :], prev[7:8, :]
            hist1 = jnp.where(row == 0, p7, pltpu.roll(g, 1, 0))
            hist2 = jnp.where(row == 0, p6, jnp.where(row == 1, p7, pltpu.roll(g, 2, 0)))
            gtail_ref[:, sl] = g[tm - 8:, :]
            gl_ref[:, sl] = g[tm - 8:, :]
        else:
            rpos = row % seq_rows
            hist1 = jnp.where(rpos == 0, st1_ref[:, sl], pltpu.roll(g, 1, 0))
            hist2 = jnp.where(rpos < 2, st2_ref[:, sl], pltpu.roll(g, 2, 0))
            gl_ref[:, sl] = g
        cw = cw_ref[:, sl]
        conv = cb_ref[:, sl] + hist2 * cw[0:1, :]
        conv = conv + hist1 * cw[1:2, :]
        conv = conv + g * cw[2:3, :]
        act = conv * (1.0 / (1.0 + jnp.exp(-conv))) * u
        y = y + _dot(act.astype(BF16), wd_ref[sl, :])
    if final:
        y = _rms(y, gf_ref[...])
    y_ref[...] = y


def _merge_ffn(x2d, oa6, ob4, oc6, w, final, tm, seq_rows=None, tiles_per_seq=1, st=None):
    t, d = x2d.shape
    d_ff = w['wg'].shape[1]
    n_chunks = 2
    nt = t // tm
    carry_mode = seq_rows is None
    rows = lambda width: pl.BlockSpec((tm, width), lambda i: (i, 0))
    slabs = lambda a: pl.BlockSpec((a.shape[0], tm, a.shape[2]), lambda i: (0, i, 0))
    ins = [x2d, oa6, ob4, oc6, w['g_out'], w['w_out'], w['g2'], w['wg'], w['wu'], w['cw'], w['cb'], w['wd'], w['g_final']]
    in_specs = [rows(d), slabs(oa6), slabs(ob4), slabs(oc6)]
    in_specs += [pl.BlockSpec(a.shape, lambda i, nd=a.ndim: (0,) * nd, pipeline_mode=pl.Buffered(1)) for a in ins[4:]]
    scratch = []
    if carry_mode:
        gl_shape, gl_spec = (nt * 8, d_ff), pl.BlockSpec((8, d_ff), lambda i: (i, 0))
        scratch = [pltpu.VMEM((8, d_ff), F32)]
    else:
        ins += list(st)
        in_specs += [rows(d_ff), rows(d_ff)]
        gl_shape, gl_spec = (t, d_ff), rows(d_ff)
    return pl.pallas_call(
        functools.partial(_merge_ffn_kernel, tm=tm, seq_rows=seq_rows, tiles_per_seq=tiles_per_seq,
                          n_chunks=n_chunks, final=final),
        grid=(nt,), in_specs=in_specs,
        out_specs=[rows(d), gl_spec],
        out_shape=[jax.ShapeDtypeStruct((t, d), F32), jax.ShapeDtypeStruct(gl_shape, F32)],
        scratch_shapes=scratch,
        compiler_params=_params("arbitrary"),
        name="merge_ffn_prompt" if carry_mode else "merge_ffn_sample",
    )(*ins)


def _sample_attn_kernel(pt_ref,
                        qx_ref, mknt_ref, mvnt_ref, slope_ref,
                        sq_ref, sknt_ref, svnt_ref, ntri_ref,
                        ql_ref, qp_ref, cn_ref, krnt_ref, wuv_ref,
                        mk_hbm, mv_hbm, sk_hbm, sv_hbm, lat_hbm, kr_hbm,
                        oa_ref, ob_ref, oc_ref,
                        mk_buf, mv_buf, sk_buf, sv_buf, lat_buf, kr_buf, sems,
                        oblk_ref,
                        *, layer, nq, n_pages, nb, past):
    n = pl.program_id(0)
    n_seq = pl.num_programs(0)
    pps = PAGES_PER_STEP
    n_steps = n_pages // pps
    hbms = (mk_hbm, mv_hbm, sk_hbm, sv_hbm, lat_hbm, kr_hbm)
    bufs = (mk_buf, mv_buf, sk_buf, sv_buf, lat_buf, kr_buf)

    def page_copies(seq, step, slot):
        out = []
        for i in range(pps):
            page = pt_ref[seq, n_pages - 1 - (step * pps + i)]
            for a in range(len(hbms)):
                out.append(pltpu.make_async_copy(hbms[a].at[layer, page], bufs[a].at[slot, i], sems.at[a, slot]))
        return out

    @pl.when(n == 0)
    def _first_fetch():
        for cp in page_copies(0, 0, 0):
            cp.start()

    a_rows = MOBA_HEADS * nq
    b_rows = SB_HEADS * nq
    c_rows = MLA_HEADS * nq
    width = MOBA_KV_HEADS * HEAD_DIM
    qx = qx_ref[...]
    qxb = qx.astype(BF16)
    slope = slope_ref[...]
    sq = sq_ref[...]
    ql = ql_ref[...].astype(BF16)
    qp = qp_ref[...].astype(BF16)
    a_lane = lax.broadcasted_iota(jnp.int32, (a_rows, LANES), 1)
    m_lane = lax.broadcasted_iota(jnp.int32, (width, LANES), 1)

    qi_b = lax.broadcasted_iota(jnp.int32, (b_rows, PAGE_SIZE), 0) % nq
    ki_b = lax.broadcasted_iota(jnp.int32, (b_rows, PAGE_SIZE), 1)
    a_new, r0 = _sb_tile(_dot(sq, sknt_ref[...].astype(BF16)), ki_b < qi_b, ntri_ref[:PAGE_SIZE, :PAGE_SIZE],
                         jnp.zeros((b_rows, 1), F32))
    sb0 = _dot_t(a_new.astype(BF16), svnt_ref[...].astype(BF16))

    def softmax_update(state, sc, values):
        m, l, acc = state
        m_new = jnp.maximum(m, jnp.max(sc, axis=1, keepdims=True))
        p = jnp.exp(sc - m_new)
        alpha = jnp.exp(m - m_new)
        return m_new, alpha * l + jnp.sum(p, axis=1, keepdims=True), alpha * acc + _dot(p.astype(BF16), values)

    n_blk = pps * PAGE_SIZE // MOBA_BLOCK
    chunk = lambda x, c: x[:, c * MOBA_BLOCK:(c + 1) * MOBA_BLOCK]

    def step(s, carry):
        r_run, sb_acc, mla, means, mblk, lblk = carry
        g = n * n_steps + s
        slot = g % 2
        for cp in page_copies(n, s, slot):
            cp.wait()
        last_of_seq = s + 1 == n_steps
        nxt_seq = jnp.minimum(jnp.where(last_of_seq, n + 1, n), n_seq - 1)
        nxt_step = jnp.where(last_of_seq, 0, s + 1)
        for cp in page_copies(nxt_seq, nxt_step, 1 - slot):
            cp.start()

        order = range(pps - 1, -1, -1)
        lanes_cat = lambda buf: jnp.concatenate([buf[slot, i] for i in order], axis=1)
        lat = lat_buf[slot].reshape(pps * PAGE_SIZE, MLA_KV_LORA).astype(BF16)
        krt = jnp.concatenate([kr_buf[slot, i] for i in range(pps)], axis=1).astype(BF16)
        skt = lanes_cat(sk_buf).astype(BF16)
        svt = lanes_cat(sv_buf).astype(BF16)
        mkt = lanes_cat(mk_buf)
        mvt = lanes_cat(mv_buf).astype(BF16)
        keys = pps * PAGE_SIZE
        blk0 = (n_pages - (s + 1) * pps) * PAGE_SIZE // MOBA_BLOCK

        c_sc = _dot_t(ql, lat) + _dot(qp, krt)
        b_z = _dot(sq, skt)
        a_sc = _dot(qxb, mkt.astype(BF16))

        sp = jnp.maximum(b_z, 0.0) + jnp.log(1.0 + jnp.exp(-jnp.abs(b_z)))
        sp_b = sp.astype(BF16)
        ntri = ntri_ref[...]
        tails = [_dot(chunk(sp_b, c), ntri) for c in range(n_blk)]
        sums = [jnp.sum(chunk(sp, c), axis=1, keepdims=True) for c in range(n_blk)]
        a_parts = [None] * n_blk
        for c in range(n_blk - 1, -1, -1):
            a_parts[c] = jnp.exp((chunk(b_z, c) - chunk(sp, c)) + tails[c] + r_run)
            r_run = r_run - sums[c]
        sb_acc = sb_acc + _dot_t(jnp.concatenate(a_parts, axis=1).astype(BF16), svt)

        mla = softmax_update(mla, c_sc, lat)

        kpos = (blk0 * MOBA_BLOCK + lax.broadcasted_iota(jnp.int32, (a_rows, keys), 1)).astype(F32)
        qpos = (past + lax.broadcasted_iota(jnp.int32, (a_rows, keys), 0) % nq).astype(F32)
        a_sc = a_sc - slope[:, :1] * (qpos - kpos)
        for c in range(n_blk):
            j = blk0 + c
            sc = chunk(a_sc, c)
            m = jnp.max(sc, axis=1, keepdims=True)
            p = jnp.exp(sc - m)
            mblk = jnp.where(a_lane == j, m, mblk)
            lblk = jnp.where(a_lane == j, jnp.sum(p, axis=1, keepdims=True), lblk)
            means = jnp.where(m_lane == j, jnp.sum(chunk(mkt, c), axis=1, keepdims=True) * (1.0 / MOBA_BLOCK), means)
            oblk_ref[j] = _dot_t(p.astype(BF16), chunk(mvt, c))
        return r_run, sb_acc, mla, means, mblk, lblk

    mla0 = (jnp.full((c_rows, 1), -jnp.inf, F32), jnp.zeros((c_rows, 1), F32), jnp.zeros((c_rows, MLA_KV_LORA), F32))
    init = (r0, sb0, mla0, jnp.zeros((width, LANES), F32), jnp.zeros((a_rows, LANES), F32), jnp.zeros((a_rows, LANES), F32))
    _, sb_acc, mla, means, mblk, lblk = lax.fori_loop(0, n_steps, step, init)
    ob_ref[...] = sb_acc

    @pl.when(n == n_seq - 1)
    def _drain():
        for cp in page_copies(n_seq - 1, 0, (n_seq * n_steps) % 2):
            cp.wait()

    cn = cn_ref[...].astype(BF16)
    qi_c = lax.broadcasted_iota(jnp.int32, (c_rows, PAGE_SIZE), 0) % nq
    ki_c = lax.broadcasted_iota(jnp.int32, (c_rows, PAGE_SIZE), 1)
    sc = _dot_t(ql, cn) + _dot(qp, krnt_ref[...].astype(BF16))
    sc = jnp.where(jnp.logical_and(ki_c <= qi_c, ki_c < nq), sc, -jnp.inf)
    _, l, acc = softmax_update(mla, sc, cn)
    full = _dot((acc / l).astype(BF16), wuv_ref[...])
    head = lax.broadcasted_iota(jnp.int32, (c_rows, MLA_V), 0) // nq
    out = jnp.zeros((c_rows, MLA_V), F32)
    for hh in range(MLA_HEADS):
        out = jnp.where(head == hh, full[:, hh * MLA_V:(hh + 1) * MLA_V], out)
    oc_ref[...] = out

    gate = _dot(qx, means, precision=HIGHEST)
    sel = _topk_mask(gate, a_lane < nb, a_lane.astype(F32), min(MOBA_TOPK, nb)) > 0.5
    qi_a = lax.broadcasted_iota(jnp.int32, (a_rows, PAGE_SIZE), 0) % nq
    ki_a = lax.broadcasted_iota(jnp.int32, (a_rows, PAGE_SIZE), 1)
    s_own = _dot(qxb, mknt_ref[...].astype(BF16)) - slope * (qi_a - ki_a).astype(F32)
    s_own = jnp.where(jnp.logical_and(qi_a >= ki_a, ki_a < nq), s_own, -jnp.inf)
    m_own = jnp.max(s_own, axis=1, keepdims=True)
    p_own = jnp.exp(s_own - m_own)
    l_own = jnp.sum(p_own, axis=1, keepdims=True)
    o_own = _dot_t(p_own.astype(BF16), mvnt_ref[...].astype(BF16))
    m_all = jnp.maximum(jnp.max(jnp.where(sel, mblk, -jnp.inf), axis=1, keepdims=True), m_own)
    w = jnp.where(sel, jnp.exp(mblk - m_all), 0.0)
    w_own = jnp.exp(m_own - m_all)
    l_all = jnp.sum(w * lblk, axis=1, keepdims=True) + w_own * l_own
    o_all = w_own * o_own
    for jj in range(nb):
        o_all = o_all + w[:, jj:jj + 1] * oblk_ref[jj]
    o_all = o_all / l_all
    first_kv = lax.broadcasted_iota(jnp.int32, (a_rows, HEAD_DIM), 0) < MOBA_GROUP * nq
    oa_ref[...] = jnp.where(first_kv, o_all[:, :HEAD_DIM], o_all[:, HEAD_DIM:])


def _sample_attn(moba_in, sb_in, mla_in, caches_t, page_table, layer, nq):
    qx, mknt, mvnt = moba_in
    sq, sknt, svnt = sb_in
    ql, qp, cn, krnt, wuv = mla_in
    n, n_pages = page_table.shape
    past = n_pages * PAGE_SIZE
    assert past % MOBA_BLOCK == 0 and PAGES_PER_STEP % 2 == 0 and n_pages % PAGES_PER_STEP == 0
    nb = past // MOBA_BLOCK
    assert nb <= LANES
    a_rows, b_rows, c_rows = MOBA_HEADS * nq, SB_HEADS * nq, MLA_HEADS * nq
    width = MOBA_KV_HEADS * HEAD_DIM
    i = jnp.arange(1, MOBA_HEADS + 1, dtype=F32)
    slope_rows = jnp.broadcast_to(jnp.repeat(jnp.exp2(-8.0 * i / MOBA_HEADS), nq)[:, None], (a_rows, LANES))

    def seq(shape):
        nd = len(shape)
        return pl.BlockSpec((None,) + shape, lambda s_, pt: (s_,) + (0,) * nd)

    def const(shape):
        nd = len(shape)
        return pl.BlockSpec(shape, lambda s_, pt: (0,) * nd)

    hbm = pl.BlockSpec(memory_space=pl.ANY)
    pps = PAGES_PER_STEP
    grid_spec = pltpu.PrefetchScalarGridSpec(
        num_scalar_prefetch=1, grid=(n,),
        in_specs=[seq((a_rows, width)), seq((width, PAGE_SIZE)), seq((width, PAGE_SIZE)), const((a_rows, LANES)),
                  seq((b_rows, HEAD_DIM)), seq((HEAD_DIM, PAGE_SIZE)), seq((HEAD_DIM, PAGE_SIZE)),
                  const((KEY_TILE, KEY_TILE)),
                  seq((c_rows, MLA_KV_LORA)), seq((c_rows, MLA_ROPE)), seq((PAGE_SIZE, MLA_KV_LORA)),
                  seq((MLA_ROPE, PAGE_SIZE)), const(wuv.shape)] + [hbm] * 6,
        out_specs=[seq((a_rows, HEAD_DIM)), seq((b_rows, HEAD_DIM)), seq((c_rows, MLA_V))],
        scratch_shapes=[pltpu.VMEM((2, pps, width, PAGE_SIZE), F32), pltpu.VMEM((2, pps, width, PAGE_SIZE), F32),
                        pltpu.VMEM((2, pps, HEAD_DIM, PAGE_SIZE), F32), pltpu.VMEM((2, pps, HEAD_DIM, PAGE_SIZE), F32),
                        pltpu.VMEM((2, pps, PAGE_SIZE, MLA_KV_LORA), F32), pltpu.VMEM((2, pps, MLA_ROPE, PAGE_SIZE), F32),
                        pltpu.SemaphoreType.DMA((6, 2)),
                        pltpu.VMEM((nb, a_rows, width), F32)])
    return pl.pallas_call(
        functools.partial(_sample_attn_kernel, layer=layer, nq=nq, n_pages=n_pages, nb=nb, past=past),
        grid_spec=grid_spec,
        out_shape=[jax.ShapeDtypeStruct((n, a_rows, HEAD_DIM), F32), jax.ShapeDtypeStruct((n, b_rows, HEAD_DIM), F32),
                   jax.ShapeDtypeStruct((n, c_rows, MLA_V), F32)],
        compiler_params=_params("arbitrary"), name="sample_attn",
    )(page_table, qx, mknt, mvnt, slope_rows, sq, sknt, svnt, _neg_tri(KEY_TILE), ql, qp, cn, krnt, wuv, *caches_t)


def _layer_weights(l, p):
    w_in = p['w_in'][l]
    wa = w_in[:, :HI_COLS]
    wa_hi = wa.astype(BF16)
    half = MLA_ROPE // 2
    kpe0 = w_in.shape[1] - MLA_ROPE
    kpe_sw = jnp.concatenate([w_in[:, kpe0 + half:], w_in[:, kpe0:kpe0 + half]], axis=1)
    wq = p['mla_w_uq'][l].reshape(MLA_Q_LORA, MLA_HEADS, MLA_QK).transpose(1, 0, 2)
    wq_sw = jnp.concatenate([wq[..., :MLA_NOPE], wq[..., MLA_NOPE + half:], wq[..., MLA_NOPE:MLA_NOPE + half]], axis=-1)
    w_uk = p['mla_w_uk'][l].transpose(1, 0, 2)
    w_uv = p['mla_w_uv'][l]
    row = lambda a: a.reshape(1, -1)
    return dict(
        g1=row(p['norm1_g'][l]), wa_hi=wa_hi, wa_lo=(wa - wa_hi.astype(F32)).astype(BF16),
        wb=jnp.concatenate([w_in[:, HI_COLS:], kpe_sw], axis=1).astype(BF16),
        qn_g=row(p['mla_q_norm_g'][l]), wq6=wq.astype(BF16), wq6s=wq_sw.astype(BF16), kvn_g=row(p['mla_kv_norm_g'][l]),
        wk6=jnp.pad(w_uk, ((0, 0), (0, 0), (0, MLA_ROPE))).astype(BF16),
        ekr=jnp.concatenate([jnp.zeros((MLA_ROPE, MLA_NOPE), F32), jnp.eye(MLA_ROPE, dtype=F32)], axis=1).astype(BF16),
        wv6t=w_uv.transpose(1, 2, 0).astype(BF16),
        wukt6=w_uk.transpose(0, 2, 1).astype(BF16),
        wuv_flat=w_uv.reshape(MLA_KV_LORA, MLA_HEADS * MLA_V).astype(BF16),
        g_out=row(p['out_norm_g'][l]), w_out=p['w_out'][l].astype(BF16), g2=row(p['norm2_g'][l]),
        wg=p['ffn_w_gate'][l].astype(BF16), wu=p['ffn_w_up'][l].astype(BF16), cw=p['ffn_conv_w'][l],
        cb=row(p['ffn_conv_b'][l]), wd=p['ffn_w_down'][l].astype(BF16), g_final=row(p['final_norm_g']))


def _rope_tables(pos):
    half = MLA_ROPE // 2
    inv = ROPE_THETA ** (-jnp.arange(half, dtype=F32) / half)
    ang = pos.astype(F32)[:, None] * inv[None, :]
    cos, sin = jnp.cos(ang), jnp.sin(ang)
    n = pos.shape[0]
    rc = jnp.concatenate([jnp.ones((n, MLA_NOPE), F32), cos, cos], axis=1)
    rs = jnp.concatenate([jnp.zeros((n, MLA_NOPE), F32), -sin, sin], axis=1)
    return rc, rs


def _prompt_forward(x, weights, depth):
    b, s, d = x.shape
    assert s % MOBA_BLOCK == 0
    t = b * s
    tm = KEY_TILE
    rc, rs = _rope_tables(jnp.arange(s))
    rc, rs = jnp.tile(rc, (b, 1)), jnp.tile(rs, (b, 1))
    x2d = x.reshape(t, d)
    st = [[] for _ in range(7)]
    for l in range(depth):
        w = weights[l]
        (qa6, ka, ka2, va, va2, qb4, kb, vb, ckv, kpe, mq6, mk6, mv6) = _project(x2d, rc, rs, w, False, tm)
        oa6 = _moba_prompt(qa6, ka2, va2, b, s)
        ob4 = _sb_prompt(qb4, kb, vb, b, s)
        oc6 = _mla_prompt(mq6, mk6, mv6, b, s)
        x2d, gl = _merge_ffn(x2d, oa6, ob4, oc6, w, l == depth - 1, tm, tiles_per_seq=s // tm)
        buf = gl.reshape(b, s // tm, 8, -1)[:, -1, 8 - (CONV_W - 1):, :]
        for lst, a in zip(st, (ka.reshape(b, s, MOBA_KV_HEADS, HEAD_DIM), va.reshape(b, s, MOBA_KV_HEADS, HEAD_DIM),
                               kb.reshape(b, s, HEAD_DIM), vb.reshape(b, s, HEAD_DIM),
                               ckv.reshape(b, s, MLA_KV_LORA), kpe.reshape(b, s, MLA_ROPE), buf)):
            lst.append(a)
    return x2d.reshape(b, s, d), [jnp.stack(a, axis=0) for a in st]


def _pad_tokens_t(a, n, nq):
    a = a.reshape(n, nq, -1).transpose(0, 2, 1)
    return jnp.pad(a, ((0, 0), (0, 0), (0, PAGE_SIZE - nq)))


def _rows_per_seq(a, n, nq):
    h = a.shape[0]
    return a.reshape(h, n, nq, -1).transpose(1, 0, 2, 3).reshape(n, h * nq, -1)


def _heads_major(a, n, nq, h):
    return a.reshape(n, h, nq, -1).transpose(1, 0, 2, 3).reshape(h, n * nq, -1)


def _sample_forward(x, caches, s_conv, page_table, weights, depth):
    n, nq, d = x.shape
    assert nq <= 8 and CONV_W - 1 <= nq
    c_mk, c_mv, c_sk, c_sv, c_mc, c_mr = caches
    n_pages = page_table.shape[1]
    past = n_pages * PAGE_SIZE
    t = n * nq
    tm = min(256, t)
    rc, rs = _rope_tables(past + jnp.arange(nq))
    rc, rs = jnp.tile(rc, (n, 1)), jnp.tile(rs, (n, 1))
    width = MOBA_KV_HEADS * HEAD_DIM
    mk_t = c_mk.transpose(0, 1, 3, 4, 2).reshape(c_mk.shape[0], c_mk.shape[1], width, PAGE_SIZE)
    mv_t = c_mv.transpose(0, 1, 3, 4, 2).reshape(c_mv.shape[0], c_mv.shape[1], width, PAGE_SIZE)
    sk_t = c_sk.transpose(0, 1, 3, 2)
    sv_t = c_sv.transpose(0, 1, 3, 2)
    mr_t = c_mr.transpose(0, 1, 3, 2)
    x2d = x.reshape(t, d)
    st = [[] for _ in range(7)]
    for l in range(depth):
        w = weights[l]
        (qa6, ka, va, qb4, kb, vb, ckv, kpe, qlat6, qpe6) = _project(x2d, rc, rs, w, True, tm)
        qa = _rows_per_seq(qa6, n, nq)
        kvh = (jnp.arange(MOBA_HEADS * nq) // (MOBA_GROUP * nq))[None, :, None]
        qx = jnp.concatenate([jnp.where(kvh == 0, qa, 0.0), jnp.where(kvh == 1, qa, 0.0)], axis=-1)
        cn = jnp.pad(ckv.reshape(n, nq, MLA_KV_LORA), ((0, 0), (0, PAGE_SIZE - nq), (0, 0)))
        oa, ob, oc = _sample_attn(
            (qx, _pad_tokens_t(ka, n, nq), _pad_tokens_t(va, n, nq)),
            (_rows_per_seq(qb4, n, nq), _pad_tokens_t(kb, n, nq), _pad_tokens_t(vb, n, nq)),
            (_rows_per_seq(qlat6, n, nq), _rows_per_seq(qpe6, n, nq), cn, _pad_tokens_t(kpe, n, nq), w['wuv_flat']),
            (mk_t, mv_t, sk_t, sv_t, c_mc, mr_t), page_table, l, nq)
        buf = s_conv[l]
        zero = jnp.zeros((n, nq - 1, buf.shape[-1]), F32)
        st1 = jnp.concatenate([buf[:, 1:2], zero], axis=1).reshape(t, -1)
        st2 = jnp.concatenate([buf, zero[:, :nq - 2]], axis=1).reshape(t, -1)
        x2d, g = _merge_ffn(x2d, _heads_major(oa, n, nq, MOBA_HEADS), _heads_major(ob, n, nq, SB_HEADS),
                            _heads_major(oc, n, nq, MLA_HEADS), w, l == depth - 1, tm, seq_rows=nq, st=(st1, st2))
        new_buf = g.reshape(n, nq, -1)[:, nq - (CONV_W - 1):, :]
        for lst, a in zip(st, (ka.reshape(n, nq, MOBA_KV_HEADS, HEAD_DIM), va.reshape(n, nq, MOBA_KV_HEADS, HEAD_DIM),
                               kb.reshape(n, nq, HEAD_DIM), vb.reshape(n, nq, HEAD_DIM),
                               ckv.reshape(n, nq, MLA_KV_LORA), kpe.reshape(n, nq, MLA_ROPE), new_buf)):
            lst.append(a)
    return x2d.reshape(n, nq, d), [jnp.stack(a, axis=0) for a in st]


def kernel(x_prompt, x_sample, cache_moba_k, cache_moba_v, cache_sb_k, cache_sb_v, cache_mla_latent, cache_mla_krope, state_ffn_conv, page_table, norm1_g, w_in, mla_q_norm_g, mla_w_uq, mla_kv_norm_g, mla_w_uk, mla_w_uv, out_norm_g, w_out, norm2_g, ffn_w_gate, ffn_w_up, ffn_conv_w, ffn_conv_b, ffn_w_down, final_norm_g):
    p = dict(norm1_g=norm1_g, w_in=w_in, mla_q_norm_g=mla_q_norm_g, mla_w_uq=mla_w_uq,
             mla_kv_norm_g=mla_kv_norm_g, mla_w_uk=mla_w_uk, mla_w_uv=mla_w_uv, out_norm_g=out_norm_g,
             w_out=w_out, norm2_g=norm2_g, ffn_w_gate=ffn_w_gate, ffn_w_up=ffn_w_up,
             ffn_conv_w=ffn_conv_w, ffn_conv_b=ffn_conv_b, ffn_w_down=ffn_w_down, final_norm_g=final_norm_g)
    depth = w_in.shape[0]
    weights = [_layer_weights(l, p) for l in range(depth)]
    y_prompt, pst = _prompt_forward(x_prompt, weights, depth)
    y_sample, sst = _sample_forward(
        x_sample, (cache_moba_k, cache_moba_v, cache_sb_k, cache_sb_v, cache_mla_latent, cache_mla_krope),
        state_ffn_conv, page_table, weights, depth)
    return (y_prompt, y_sample, *pst, *sst)
```
